```python
import jax, jax.numpy as jnp
from jax import lax
import numpy as np

D_MODEL = 1024
BATCH = 8
SEQ = 2048
DEPTH = 4
DEC_BATCH = 128
DEC_SEQ = 4
PAST_LEN = 16384
PAGE_SIZE = 128

CHUNK = 128
D_A = D_MODEL // 2
N_A_GROUPS = 4
A_GROUP = D_A // N_A_GROUPS
D_B = D_MODEL // 2
CONV_W = 3
D_C = D_MODEL // 2
POOL_WINDOWS = (2, 4, 8, 16)
N_C_GROUPS = len(POOL_WINDOWS)
C_GROUP = D_C // N_C_GROUPS
POOL_BUF = max(POOL_WINDOWS) - 1
N_BRANCH = 3
D_FF = 2816
EPS = 1e-6
SPLITS = (D_A, 2 * D_A, 2 * D_A + D_B, 2 * D_A + 2 * D_B, 2 * D_A + 3 * D_B, 2 * D_A + 3 * D_B + D_C)
D_IN = 2 * D_A + 3 * D_B + D_C + N_BRANCH * D_MODEL

kernel_name = "hybrid_gmlp_conv_pool_decoder_step"


def rms_norm(x, g):
    xf = x.astype(jnp.float32)
    y = xf * lax.rsqrt(jnp.mean(xf * xf, axis=-1, keepdims=True) + EPS)
    return (y * g.astype(jnp.float32)).astype(x.dtype)


def layer_norm(x, g, b):
    xf = x.astype(jnp.float32)
    mu = jnp.mean(xf, axis=-1, keepdims=True)
    xc = xf - mu
    y = xc * lax.rsqrt(jnp.mean(xc * xc, axis=-1, keepdims=True) + EPS)
    return (y * g.astype(jnp.float32) + b.astype(jnp.float32)).astype(x.dtype)


def swiglu(x, w_gu, w_down):
    g, u = jnp.split(x @ w_gu, 2, axis=-1)
    return (jax.nn.silu(g) * u) @ w_down


def chunk_spatial_gate(u, v, w_s, b_s):
    bn, L, _ = v.shape
    c = min(L, CHUNK)
    n = L // c
    mask = jnp.tril(jnp.ones((c, c), dtype=bool))
    ws = jnp.where(mask[None], w_s[:, :c, :c], jnp.zeros((), w_s.dtype)).astype(v.dtype)
    vr = v.reshape(bn, n, c, N_A_GROUPS, A_GROUP)
    mixed = jnp.einsum('gts,bnsgc->bntgc', ws, vr) + b_s[:, :c].T[None, None, :, :, None].astype(v.dtype)
    return u * mixed.reshape(bn, L, D_A)


def short_conv(z_ext, w_conv, L):
    return sum(w_conv[k] * z_ext[:, k:k + L] for k in range(CONV_W))


def multi_pool(p_ext, start_pos, L):
    pf = p_ext.astype(jnp.float32)
    cs = jnp.concatenate([jnp.zeros_like(pf[:, :1]), jnp.cumsum(pf, axis=1)], axis=1)
    pos = start_pos + jnp.arange(L)
    outs = []
    for g, w in enumerate(POOL_WINDOWS):
        sl = slice(g * C_GROUP, (g + 1) * C_GROUP)
        hi = cs[:, POOL_BUF + 1:POOL_BUF + 1 + L, sl]
        lo = cs[:, POOL_BUF + 1 - w:POOL_BUF + 1 - w + L, sl]
        cnt = jnp.minimum(pos + 1, w).astype(jnp.float32)[None, :, None]
        outs.append((hi - lo) / cnt)
    return jnp.concatenate(outs, axis=-1).astype(p_ext.dtype)


def mixer(h, conv_buf, pool_buf, start_pos, w_in, w_s, b_s, ln_g, ln_b, w_conv, w_pool,
          pool_scale, w_a_out, w_b_out, w_c_out, w_o):
    bn, L, _ = h.shape
    u, v, xb, gb, gc, p, gates = jnp.split(h @ w_in, SPLITS, axis=-1)
    u = jax.nn.gelu(u)
    v = layer_norm(jax.nn.gelu(v), ln_g, ln_b)
    y_a = chunk_spatial_gate(u, v, w_s, b_s) @ w_a_out
    z_ext = jnp.concatenate([conv_buf, gc * xb], axis=1)
    y_b = (gb * short_conv(z_ext, w_conv, L)) @ w_b_out
    p_ext = jnp.concatenate([pool_buf, p], axis=1)
    d = (multi_pool(p_ext, start_pos, L) - p).reshape(bn, L, N_C_GROUPS, C_GROUP)
    d = jnp.einsum('blgc,gcd->blgd', d, w_pool).reshape(bn, L, D_C) * pool_scale
    y_c = d @ w_c_out
    g_a, g_b, g_c = jnp.split(jax.nn.sigmoid(gates), N_BRANCH, axis=-1)
    out = (g_a * y_a + g_b * y_b + g_c * y_c) @ w_o
    return out, z_ext[:, -(CONV_W - 1):], p_ext[:, -POOL_BUF:], v


def run_trunk(x, conv_bufs, pool_bufs, start_pos, norm_g, w_ffn_gu, w_ffn_down, w_in, w_s, b_s,
              ln_g, ln_b, w_conv, w_pool, pool_scale, w_a_out, w_b_out, w_c_out, w_o):
    new_conv, new_pool, new_v = [], [], []
    for l in range(DEPTH):
        g = norm_g[l]
        x = x + 0.5 * rms_norm(swiglu(rms_norm(x, g[0]), w_ffn_gu[l, 0], w_ffn_down[l, 0]), g[1])
        m, cb, pb, v = mixer(rms_norm(x, g[2]), conv_bufs[l], pool_bufs[l], start_pos, w_in[l],
                             w_s[l], b_s[l], ln_g[l], ln_b[l], w_conv[l], w_pool[l], pool_scale[l],
                             w_a_out[l], w_b_out[l], w_c_out[l], w_o[l])
        x = x + rms_norm(m, g[3])
        x = x + 0.5 * rms_norm(swiglu(rms_norm(x, g[4]), w_ffn_gu[l, 1], w_ffn_down[l, 1]), g[5])
        new_conv.append(cb)
        new_pool.append(pb)
        new_v.append(v)
    return x, jnp.stack(new_conv), jnp.stack(new_pool), jnp.stack(new_v)


def setup_inputs(seed: int = 0) -> dict:
    key = jax.random.key(seed)
    ks = jax.random.split(key, 20)
    nrm = lambda k, shape, fan_in: jax.random.normal(k, shape, jnp.float32) * (fan_in ** -0.5)
    return {
        "x_prompt": jax.random.normal(ks[0], (BATCH, SEQ, D_MODEL), jnp.float32),
        "x_sample": jax.random.normal(ks[1], (DEC_BATCH, DEC_SEQ, D_MODEL), jnp.float32),
        "state_conv": jax.random.normal(ks[2], (DEPTH, DEC_BATCH, CONV_W - 1, D_B), jnp.float32),
        "state_pool": jax.random.normal(ks[3], (DEPTH, DEC_BATCH, POOL_BUF, D_C), jnp.float32),
        "norm_g": 1.0 + 0.02 * jax.random.normal(ks[4], (DEPTH, 6, D_MODEL), jnp.float32),
        "w_ffn_gu": nrm(ks[5], (DEPTH, 2, D_MODEL, 2 * D_FF), D_MODEL),
        "w_ffn_down": nrm(ks[6], (DEPTH, 2, D_FF, D_MODEL), D_FF),
        "w_in": nrm(ks[7], (DEPTH, D_MODEL, D_IN), D_MODEL),
        "w_s": nrm(ks[8], (DEPTH, N_A_GROUPS, CHUNK, CHUNK), CHUNK),
        "b_s": 1.0 + 0.02 * jax.random.normal(ks[9], (DEPTH, N_A_GROUPS, CHUNK), jnp.float32),
        "ln_g": 1.0 + 0.02 * jax.random.normal(ks[10], (DEPTH, D_A), jnp.float32),
        "ln_b": 0.02 * jax.random.normal(ks[11], (DEPTH, D_A), jnp.float32),
        "w_conv": nrm(ks[12], (DEPTH, CONV_W, D_B), CONV_W),
        "w_pool": nrm(ks[13], (DEPTH, N_C_GROUPS, C_GROUP, C_GROUP), C_GROUP),
        "pool_scale": 1.0 + 0.02 * jax.random.normal(ks[14], (DEPTH, D_C), jnp.float32),
        "w_a_out": nrm(ks[15], (DEPTH, D_A, D_MODEL), D_A),
        "w_b_out": nrm(ks[16], (DEPTH, D_B, D_MODEL), D_B),
        "w_c_out": nrm(ks[17], (DEPTH, D_C, D_MODEL), D_C),
        "w_o": nrm(ks[18], (DEPTH, D_MODEL, D_MODEL), D_MODEL),
    }


def reference(x_prompt, x_sample, state_conv, state_pool, norm_g, w_ffn_gu, w_ffn_down, w_in, w_s,
              b_s, ln_g, ln_b, w_conv, w_pool, pool_scale, w_a_out, w_b_out, w_c_out, w_o):
    zero_conv = jnp.zeros((DEPTH, x_prompt.shape[0], CONV_W - 1, D_B), x_prompt.dtype)
    zero_pool = jnp.zeros((DEPTH, x_prompt.shape[0], POOL_BUF, D_C), x_prompt.dtype)
    y_prompt, new_conv_prompt, new_pool_prompt, _ = run_trunk(
        x_prompt, zero_conv, zero_pool, 0, norm_g, w_ffn_gu, w_ffn_down, w_in, w_s, b_s, ln_g, ln_b,
        w_conv, w_pool, pool_scale, w_a_out, w_b_out, w_c_out, w_o)
    y_sample, new_conv_sample, new_pool_sample, new_chunk_v_sample = run_trunk(
        x_sample, state_conv.astype(x_sample.dtype), state_pool.astype(x_sample.dtype), PAST_LEN,
        norm_g, w_ffn_gu, w_ffn_down, w_in, w_s, b_s, ln_g, ln_b, w_conv, w_pool, pool_scale,
        w_a_out, w_b_out, w_c_out, w_o)
    return (y_prompt, y_sample, new_conv_prompt, new_pool_prompt, new_conv_sample, new_pool_sample, new_chunk_v_sample)
```

```python
import functools

import jax
import jax.numpy as jnp
from jax import lax
from jax.experimental import pallas as pl
from jax.experimental.pallas import tpu as pltpu

EPS = 1e-6
PAST_LEN = 16384
CHUNK = 128
N_GROUPS = 4
POOL_WINDOWS = (2, 4, 8, 16)
POOL_BUF = max(POOL_WINDOWS) - 1
CONV_W = 3
N_BRANCH = 3

SUBLANES = 8
LANES = 128
HALO = 16
FFN_COL_CHUNK = 512
ROW_TILE = 512
VMEM_LIMIT_BYTES = 56 * 1024 * 1024

F32 = jnp.float32
BF16 = jnp.bfloat16


def _rms(x, g):
    return x * lax.rsqrt(jnp.mean(x * x, axis=-1, keepdims=True) + EPS) * g


def _layer_norm(x, g, b):
    xc = x - jnp.mean(x, axis=-1, keepdims=True)
    return xc * lax.rsqrt(jnp.mean(xc * xc, axis=-1, keepdims=True) + EPS) * g + b


def _dot(a, b):
    return jnp.dot(a, b, preferred_element_type=F32)


def _resident(block_shape, index_map):
    return pl.BlockSpec(block_shape, index_map, pipeline_mode=pl.Buffered(1))


def _ffn_body(x_ref, ng_ref, wgu_ref, wdown_ref, o_ref, *, d_ff, pre, post):
    x = x_ref[...]
    ng = ng_ref[...]
    h = _rms(x, ng[pre:pre + 1]).astype(BF16)
    acc = None
    for c0 in range(0, d_ff, FFN_COL_CHUNK):
        c1 = min(c0 + FFN_COL_CHUNK, d_ff)
        g = _dot(h, wgu_ref[:, c0:c1])
        u = _dot(h, wgu_ref[:, d_ff + c0:d_ff + c1])
        a = (g * jax.nn.sigmoid(g) * u).astype(BF16)
        part = _dot(a, wdown_ref[c0:c1, :])
        acc = part if acc is None else acc + part
    o_ref[...] = x + 0.5 * _rms(acc, ng[post:post + 1])


def _ffn(x, norm_g, w_gu, w_down, layer, which):
    rows, d = x.shape
    d_ff = w_down.shape[2]
    tm = min(ROW_TILE, rows)
    assert rows % tm == 0
    body = functools.partial(_ffn_body, d_ff=d_ff, pre=4 * which, post=4 * which + 1)
    return pl.pallas_call(
        body,
        grid=(rows // tm,),
        in_specs=[
            pl.BlockSpec((tm, d), lambda i: (i, 0)),
            _resident((None,) + norm_g.shape[1:], lambda i: (layer, 0, 0)),
            _resident((None, None) + w_gu.shape[2:], lambda i: (layer, which, 0, 0)),
            _resident((None, None) + w_down.shape[2:], lambda i: (layer, which, 0, 0)),
        ],
        out_specs=pl.BlockSpec((tm, d), lambda i: (i, 0)),
        out_shape=jax.ShapeDtypeStruct((rows, d), F32),
        compiler_params=pltpu.CompilerParams(
            dimension_semantics=("arbitrary",), vmem_limit_bytes=VMEM_LIMIT_BYTES),
        name=f"ffn{which}_l{layer}_r{rows}",
    )(x, norm_g, w_gu, w_down)


def _project(h, win_ref, lo, hi):
    return _dot(h, win_ref[:, lo:hi])


def _branch_c_tail(d_groups, wpool_ref, pscale_ref):
    outs = [_dot(d_groups[g].astype(BF16), wpool_ref[g]) for g in range(N_GROUPS)]
    return jnp.concatenate(outs, axis=-1) * pscale_ref[...]


def _merge_and_project(h, win_ref, gate0, d_model, ya, yb, yc, wa_ref, wb_ref, wc_ref, wo_ref):
    merged = None
    for k, (y, w_ref) in enumerate(((ya, wa_ref), (yb, wb_ref), (yc, wc_ref))):
        gate = jax.nn.sigmoid(_project(h, win_ref, gate0 + k * d_model, gate0 + (k + 1) * d_model))
        term = gate * _dot(y.astype(BF16), w_ref[...])
        merged = term if merged is None else merged + term
    return _dot(merged.astype(BF16), wo_ref[...])


def _mixer_prompt_body(x_ref, ng_ref, win_ref, ws_ref, bst_ref, lng_ref, lnb_ref, wconv_ref,
                       wpool_ref, pscale_ref, wa_ref, wb_ref, wc_ref, wo_ref,
                       o_ref, conv_out_ref, pool_out_ref,
                       zext_ref, pext_ref, ya_ref, *, ts, d_model, d_br):
    j = pl.program_id(1)
    last_j = pl.num_programs(1) - 1
    gw = d_br // N_GROUPS

    @pl.when(j == 0)
    def _():
        zext_ref[0:HALO, :] = jnp.zeros((HALO, d_br), F32)
        pext_ref[0:HALO, :] = jnp.zeros((HALO, d_br), F32)

    x = x_ref[...]
    ng = ng_ref[...]
    h = _rms(x, ng[2:3]).astype(BF16)

    u = jax.nn.gelu(_project(h, win_ref, 0, d_br))
    v = _layer_norm(jax.nn.gelu(_project(h, win_ref, d_br, 2 * d_br)), lng_ref[...], lnb_ref[...])
    vb = v.astype(BF16)
    rows_i = lax.broadcasted_iota(jnp.int32, (CHUNK, CHUNK), 0)
    cols_i = lax.broadcasted_iota(jnp.int32, (CHUNK, CHUNK), 1)
    causal = cols_i <= rows_i
    for g in range(N_GROUPS):
        wsg = jnp.where(causal, ws_ref[g], 0.0).astype(BF16)
        bias = bst_ref[:, g:g + 1]
        for n in range(ts // CHUNK):
            r0, c0 = n * CHUNK, g * gw
            mixed = _dot(wsg, vb[r0:r0 + CHUNK, c0:c0 + gw]) + bias
            ya_ref[r0:r0 + CHUNK, c0:c0 + gw] = u[r0:r0 + CHUNK, c0:c0 + gw] * mixed
    ya = ya_ref[...]

    z = _project(h, win_ref, 4 * d_br, 5 * d_br) * _project(h, win_ref, 2 * d_br, 3 * d_br)
    zext_ref[HALO:HALO + ts, :] = z
    wconv = wconv_ref[...]
    conv = wconv[0:1] * zext_ref[HALO - 2:HALO - 2 + ts, :]
    conv = conv + wconv[1:2] * zext_ref[HALO - 1:HALO - 1 + ts, :]
    conv = conv + wconv[2:3] * z
    yb = _project(h, win_ref, 3 * d_br, 4 * d_br) * conv

    p = _project(h, win_ref, 5 * d_br, 6 * d_br)
    pext_ref[HALO:HALO + ts, :] = p
    pos = j * ts + lax.broadcasted_iota(jnp.int32, (ts, gw), 0)
    d_groups = []
    for g, w in enumerate(POOL_WINDOWS):
        c0 = g * gw
        s = pext_ref[HALO:HALO + ts, c0:c0 + gw]
        for k in range(1, w):
            s = s + pext_ref[HALO - k:HALO - k + ts, c0:c0 + gw]
        cnt = jnp.minimum(pos + 1, w).astype(F32)
        d_groups.append(s / cnt - p[:, c0:c0 + gw])
    yc = _branch_c_tail(d_groups, wpool_ref, pscale_ref)

    out = _merge_and_project(h, win_ref, 6 * d_br, d_model, ya, yb, yc, wa_ref, wb_ref, wc_ref, wo_ref)
    o_ref[...] = x + _rms(out, ng[3:4])

    @pl.when(j == last_j)
    def _():
        conv_out_ref[...] = zext_ref[HALO + ts - (CONV_W - 1):HALO + ts, :]
        pool_out_ref[...] = pext_ref[HALO + ts - POOL_BUF:HALO + ts, :]

    zext_ref[0:HALO, :] = zext_ref[ts:ts + HALO, :]
    pext_ref[0:HALO, :] = pext_ref[ts:ts + HALO, :]


def _mixer_prompt(x, batch, norm_g, w_in, w_s, b_s_t, ln_g, ln_b, w_conv, w_pool, pool_scale,
                  w_a, w_b, w_c, w_o, layer):
    rows, d_model = x.shape
    seq = rows // batch
    d_br = w_a.shape[1]
    ts = min(ROW_TILE, seq)
    assert seq % ts == 0 and ts % CHUNK == 0 and ts >= HALO
    nj = seq // ts
    body = functools.partial(_mixer_prompt_body, ts=ts, d_model=d_model, d_br=d_br)

    def whole(a):
        nd = a.ndim - 1
        return _resident((None,) + a.shape[1:], lambda b, j, nd=nd: (layer,) + (0,) * nd)

    return pl.pallas_call(
        body,
        grid=(batch, nj),
        in_specs=[pl.BlockSpec((ts, d_model), lambda b, j: (b * nj + j, 0))]
        + [whole(a) for a in (norm_g, w_in, w_s, b_s_t, ln_g, ln_b, w_conv, w_pool, pool_scale,
                              w_a, w_b, w_c, w_o)],
        out_specs=[
            pl.BlockSpec((ts, d_model), lambda b, j: (b * nj + j, 0)),
            pl.BlockSpec((None, CONV_W - 1, d_br), lambda b, j: (b, 0, 0)),
            pl.BlockSpec((None, POOL_BUF, d_br), lambda b, j: (b, 0, 0)),
        ],
        out_shape=[
            jax.ShapeDtypeStruct((rows, d_model), F32),
            jax.ShapeDtypeStruct((batch, CONV_W - 1, d_br), F32),
            jax.ShapeDtypeStruct((batch, POOL_BUF, d_br), F32),
        ],
        scratch_shapes=[
            pltpu.VMEM((HALO + ts, d_br), F32),
            pltpu.VMEM((HALO + ts, d_br), F32),
            pltpu.VMEM((ts, d_br), F32),
        ],
        compiler_params=pltpu.CompilerParams(
            dimension_semantics=("arbitrary", "arbitrary"), vmem_limit_bytes=VMEM_LIMIT_BYTES),
        name=f"mixer_prompt_l{layer}",
    )(x, norm_g, w_in, w_s, b_s_t, ln_g, ln_b, w_conv, w_pool, pool_scale, w_a, w_b, w_c, w_o)


def _mixer_sample_body(ws_ref, bs_ref, x_ref, conv_ref, pool_ref, ng_ref, win_ref, lng_ref, lnb_ref,
                       wconv_ref, wpool_ref, pscale_ref, wa_ref, wb_ref, wc_ref, wo_ref,
                       o_ref, v_out_ref, conv_out_ref, pool_out_ref,
                       *, layer, nb, steps, d_model, d_br, start_pos):
    gw = d_br // N_GROUPS
    x = x_ref[...]
    ng = ng_ref[...]
    h = _rms(x, ng[2:3]).astype(BF16)

    def blk(a, i):
        return a[i * nb:(i + 1) * nb]

    u = jax.nn.gelu(_project(h, win_ref, 0, d_br))
    v = _layer_norm(jax.nn.gelu(_project(h, win_ref, d_br, 2 * d_br)), lng_ref[...], lnb_ref[...])
    v_out_ref[...] = v
    ya_rows = []
    for t in range(steps):
        cols = []
        for g in range(N_GROUPS):
            c0 = g * gw
            mixed = None
            for s in range(t + 1):
                term = ws_ref[layer, g * steps * steps + t * steps + s] * blk(v, s)[:, c0:c0 + gw]
                mixed = term if mixed is None else mixed + term
            cols.append(mixed + bs_ref[layer, g * steps + t])
        ya_rows.append(blk(u, t) * jnp.concatenate(cols, axis=-1))
    ya = jnp.concatenate(ya_rows, axis=0)

    z = _project(h, win_ref, 4 * d_br, 5 * d_br) * _project(h, win_ref, 2 * d_br, 3 * d_br)
    z_ext = [conv_ref[k] for k in range(CONV_W - 1)] + [blk(z, i) for i in range(steps)]
    wconv = wconv_ref[...]
    conv_rows = []
    for i in range(steps):
        c = wconv[0:1] * z_ext[i]
        for k in range(1, CONV_W):
            c = c + wconv[k:k + 1] * z_ext[i + k]
        conv_rows.append(c)
    yb = _project(h, win_ref, 3 * d_br, 4 * d_br) * jnp.concatenate(conv_rows, axis=0)
    for k in range(CONV_W - 1):
        conv_out_ref[k] = z_ext[steps + k]

    p = _project(h, win_ref, 5 * d_br, 6 * d_br)
    p_ext = [pool_ref[k] for k in range(POOL_BUF)] + [blk(p, i) for i in range(steps)]
    d_groups = []
    for g, w in enumerate(POOL_WINDOWS):
        c0 = g * gw
        rows_g = []
        for i in range(steps):
            s = p_ext[POOL_BUF + i][:, c0:c0 + gw]
            for k in range(1, w):
                s = s + p_ext[POOL_BUF + i - k][:, c0:c0 + gw]
            cnt = float(min(start_pos + i + 1, w))
            rows_g.append(s / cnt - p_ext[POOL_BUF + i][:, c0:c0 + gw])
        d_groups.append(jnp.concatenate(rows_g, axis=0))
    yc = _branch_c_tail(d_groups, wpool_ref, pscale_ref)
    for k in range(POOL_BUF):
        pool_out_ref[k] = p_ext[steps + k]

    out = _merge_and_project(h, win_ref, 6 * d_br, d_model, ya, yb, yc, wa_ref, wb_ref, wc_ref, wo_ref)
    o_ref[...] = x + _rms(out, ng[3:4])


def _mixer_sample(x, conv_state, pool_state, steps, norm_g, w_in, ws_small, bs_small, ln_g, ln_b,
                  w_conv, w_pool, pool_scale, w_a, w_b, w_c, w_o, layer):
    rows, d_model = x.shape
    nb = rows // steps
    d_br = w_a.shape[1]
    assert steps <= CHUNK and nb % SUBLANES == 0
    body = functools.partial(_mixer_sample_body, layer=layer, nb=nb, steps=steps, d_model=d_model,
                             d_br=d_br, start_pos=PAST_LEN)

    def whole(a):
        nd = a.ndim - 1
        return _resident((None,) + a.shape[1:], lambda i, nd=nd: (layer,) + (0,) * nd)

    smem = pl.BlockSpec(memory_space=pltpu.SMEM)

    return pl.pallas_call(
        body,
        grid=(1,),
        in_specs=[smem, smem,
                  pl.BlockSpec((rows, d_model), lambda i: (0, 0)),
                  whole(conv_state), whole(pool_state)]
        + [whole(a) for a in (norm_g, w_in, ln_g, ln_b, w_conv, w_pool, pool_scale, w_a, w_b, w_c, w_o)],
        out_specs=[
            pl.BlockSpec((rows, d_model), lambda i: (0, 0)),
            pl.BlockSpec((rows, d_br), lambda i: (0, 0)),
            pl.BlockSpec((CONV_W - 1, nb, d_br), lambda i: (0, 0, 0)),
            pl.BlockSpec((POOL_BUF, nb, d_br), lambda i: (0, 0, 0)),
        ],
        out_shape=[
            jax.ShapeDtypeStruct((rows, d_model), F32),
            jax.ShapeDtypeStruct((rows, d_br), F32),
            jax.ShapeDtypeStruct((CONV_W - 1, nb, d_br), F32),
            jax.ShapeDtypeStruct((POOL_BUF, nb, d_br), F32),
        ],
        compiler_params=pltpu.CompilerParams(
            dimension_semantics=("arbitrary",), vmem_limit_bytes=VMEM_LIMIT_BYTES),
        name=f"mixer_sample_l{layer}",
    )(ws_small, bs_small, x, conv_state, pool_state, norm_g, w_in, ln_g, ln_b, w_conv, w_pool,
      pool_scale, w_a, w_b, w_c, w_o)


def kernel(x_prompt, x_sample, state_conv, state_pool, norm_g, w_ffn_gu, w_ffn_down, w_in, w_s, b_s,
           ln_g, ln_b, w_conv, w_pool, pool_scale, w_a_out, w_b_out, w_c_out, w_o):
    depth = w_in.shape[0]
    batch, seq, d_model = x_prompt.shape
    nb, steps, _ = x_sample.shape

    wgu, wdown, win = w_ffn_gu.astype(BF16), w_ffn_down.astype(BF16), w_in.astype(BF16)
    wpool = w_pool.astype(BF16)
    wa, wb, wc, wo = (w.astype(BF16) for w in (w_a_out, w_b_out, w_c_out, w_o))
    lng, lnb, pscale = (a[:, None, :] for a in (ln_g, ln_b, pool_scale))
    b_s_t = jnp.swapaxes(b_s, 1, 2)
    ws_small = w_s[:, :, :steps, :steps].reshape(depth, -1)
    bs_small = b_s[:, :, :steps].reshape(depth, -1)

    xp = x_prompt.reshape(batch * seq, d_model)
    xs = jnp.swapaxes(x_sample, 0, 1).reshape(steps * nb, d_model)
    conv_s = jnp.swapaxes(state_conv, 1, 2)
    pool_s = jnp.swapaxes(state_pool, 1, 2)

    conv_p_out, pool_p_out, conv_s_out, pool_s_out, v_s_out = [], [], [], [], []
    for l in range(depth):
        xp = _ffn(xp, norm_g, wgu, wdown, l, 0)
        xs = _ffn(xs, norm_g, wgu, wdown, l, 0)
        xp, cp, pp = _mixer_prompt(xp, batch, norm_g, win, w_s, b_s_t, lng, lnb, w_conv, wpool, pscale,
                                   wa, wb, wc, wo, l)
        xs, vs, cs, ps = _mixer_sample(xs, conv_s, pool_s, steps, norm_g, win, ws_small, bs_small, lng,
                                       lnb, w_conv, wpool, pscale, wa, wb, wc, wo, l)
        xp = _ffn(xp, norm_g, wgu, wdown, l, 1)
        xs = _ffn(xs, norm_g, wgu, wdown, l, 1)
        conv_p_out.append(cp)
        pool_p_out.append(pp)
        conv_s_out.append(jnp.swapaxes(cs, 0, 1))
        pool_s_out.append(jnp.swapaxes(ps, 0, 1))
        v_s_out.append(jnp.swapaxes(vs.reshape(steps, nb, -1), 0, 1))

    y_prompt = xp.reshape(batch, seq, d_model)
    y_sample = jnp.swapaxes(xs.reshape(steps, nb, d_model), 0, 1)
    return (y_prompt, y_sample, jnp.stack(conv_p_out), jnp.stack(pool_p_out),
            jnp.stack(conv_s_out), jnp.stack(pool_s_out), jnp.stack(v_s_out))
```

```python
import functools

import jax
import jax.numpy as jnp
from jax import lax
from jax.experimental import pallas as pl
from jax.experimental.pallas import tpu as pltpu

EPS = 1e-6
PAST_LEN = 16384
CHUNK = 128
N_GROUPS = 4
POOL_WINDOWS = (2, 4, 8, 16)
POOL_BUF = max(POOL_WINDOWS) - 1
CONV_W = 3
N_BRANCH = 3

SUBLANES = 8
LANES = 128
HALO = 16
FFN_COL_CHUNK = 512
ROW_TILE = 512
VMEM_LIMIT_BYTES = 56 * 1024 * 1024

F32 = jnp.float32
BF16 = jnp.bfloat16


def _rms(x, g):
    return x * lax.rsqrt(jnp.mean(x * x, axis=-1, keepdims=True) + EPS) * g


def _layer_norm(x, g, b):
    xc = x - jnp.mean(x, axis=-1, keepdims=True)
    return xc * lax.rsqrt(jnp.mean(xc * xc, axis=-1, keepdims=True) + EPS) * g + b


def _dot(a, b):
    return jnp.dot(a, b, preferred_element_type=F32)


def _resident(block_shape, index_map):
    return pl.BlockSpec(block_shape, index_map, pipeline_mode=pl.Buffered(1))


def _ffn_body(x_ref, xprev_ref, ng_ref, wgu_ref, wdown_ref, o_ref, acc_ref, alast_ref,
              *, d_ff, pre, post):
    s = pl.program_id(0)
    n_tiles = pl.num_programs(0) - 1
    ng = ng_ref[...]
    chunks = [(c0, min(c0 + FFN_COL_CHUNK, d_ff)) for c0 in range(0, d_ff, FFN_COL_CHUNK)]

    def gated(h, c0, c1):
        g = _dot(h, wgu_ref[:, c0:c1])
        u = _dot(h, wgu_ref[:, d_ff + c0:d_ff + c1])
        return (g * jax.nn.sigmoid(g) * u).astype(BF16)

    def finish_previous():
        c0, c1 = chunks[-1]
        y = acc_ref[...] + _dot(alast_ref[...], wdown_ref[c0:c1, :])
        o_ref[...] = xprev_ref[...] + 0.5 * _rms(y, ng[post:post + 1])

    @pl.when(s == 0)
    def _():
        acc_ref[...] = jnp.zeros(acc_ref.shape, F32)
        alast_ref[...] = jnp.zeros(alast_ref.shape, BF16)

    @pl.when(s < n_tiles)
    def _():
        finish_previous()
        h = _rms(x_ref[...], ng[pre:pre + 1]).astype(BF16)
        acc = None
        for c0, c1 in chunks[:-1]:
            part = _dot(gated(h, c0, c1), wdown_ref[c0:c1, :])
            acc = part if acc is None else acc + part
        alast_ref[...] = gated(h, *chunks[-1])
        acc_ref[...] = acc

    @pl.when(s == n_tiles)
    def _():
        finish_previous()


def _ffn(x, norm_g, w_gu, w_down, layer, which):
    rows, d = x.shape
    d_ff = w_down.shape[2]
    tm = min(ROW_TILE, rows)
    assert rows % tm == 0
    n_tiles = rows // tm
    assert d_ff > FFN_COL_CHUNK
    last_chunk = d_ff - (d_ff - 1) // FFN_COL_CHUNK * FFN_COL_CHUNK
    body = functools.partial(_ffn_body, d_ff=d_ff, pre=4 * which, post=4 * which + 1)
    return pl.pallas_call(
        body,
        grid=(n_tiles + 1,),
        in_specs=[
            pl.BlockSpec((tm, d), lambda s: (jnp.minimum(s, n_tiles - 1), 0)),
            pl.BlockSpec((tm, d), lambda s: (jnp.maximum(s - 1, 0), 0)),
            _resident((None,) + norm_g.shape[1:], lambda s: (layer, 0, 0)),
            _resident((None, None) + w_gu.shape[2:], lambda s: (layer, which, 0, 0)),
            _resident((None, None) + w_down.shape[2:], lambda s: (layer, which, 0, 0)),
        ],
        out_specs=pl.BlockSpec((tm, d), lambda s: (jnp.maximum(s - 1, 0), 0)),
        out_shape=jax.ShapeDtypeStruct((rows, d), F32),
        scratch_shapes=[pltpu.VMEM((tm, d), F32), pltpu.VMEM((tm, last_chunk), BF16)],
        compiler_params=pltpu.CompilerParams(
            dimension_semantics=("arbitrary",), vmem_limit_bytes=VMEM_LIMIT_BYTES),
        name=f"ffn{which}_l{layer}_r{rows}",
    )(x, x, norm_g, w_gu, w_down)


def _project(h, win_ref, lo, hi):
    return _dot(h, win_ref[:, lo:hi])


def _branch_c_tail(d_groups, wpool_ref, pscale_ref):
    outs = [_dot(d_groups[g].astype(BF16), wpool_ref[g]) for g in range(N_GROUPS)]
    return jnp.concatenate(outs, axis=-1) * pscale_ref[...]


def _merge_and_project(h, win_ref, gate0, d_model, ya, yb, yc, wa_ref, wb_ref, wc_ref, wo_ref):
    merged = None
    for k, (y, w_ref) in enumerate(((ya, wa_ref), (yb, wb_ref), (yc, wc_ref))):
        gate = jax.nn.sigmoid(_project(h, win_ref, gate0 + k * d_model, gate0 + (k + 1) * d_model))
        term = gate * _dot(y.astype(BF16), w_ref[...])
        merged = term if merged is None else merged + term
    return _dot(merged.astype(BF16), wo_ref[...])


def _mixer_prompt_body(x_ref, ng_ref, win_ref, ws_ref, bst_ref, lng_ref, lnb_ref, wconv_ref,
                       wpool_ref, pscale_ref, wa_ref, wb_ref, wc_ref, wo_ref,
                       o_ref, conv_out_ref, pool_out_ref,
                       zext_ref, pext_ref, ya_ref, *, ts, d_model, d_br):
    j = pl.program_id(1)
    last_j = pl.num_programs(1) - 1
    gw = d_br // N_GROUPS

    @pl.when(j == 0)
    def _():
        zext_ref[0:HALO, :] = jnp.zeros((HALO, d_br), F32)
        pext_ref[0:HALO, :] = jnp.zeros((HALO, d_br), F32)

    x = x_ref[...]
    ng = ng_ref[...]
    h = _rms(x, ng[2:3]).astype(BF16)

    v_raw = _project(h, win_ref, d_br, 2 * d_br)
    u_raw = _project(h, win_ref, 0, d_br)
    xb = _project(h, win_ref, 2 * d_br, 3 * d_br)
    gb = _project(h, win_ref, 3 * d_br, 4 * d_br)
    gc = _project(h, win_ref, 4 * d_br, 5 * d_br)
    p = _project(h, win_ref, 5 * d_br, 6 * d_br)

    def gated(y, w_ref, k):
        lo = 6 * d_br + k * d_model
        gate = jax.nn.sigmoid(_project(h, win_ref, lo, lo + d_model))
        return gate * _dot(y.astype(BF16), w_ref[...])

    u = jax.nn.gelu(u_raw)
    vb = _layer_norm(jax.nn.gelu(v_raw), lng_ref[...], lnb_ref[...]).astype(BF16)
    rows_i = lax.broadcasted_iota(jnp.int32, (CHUNK, CHUNK), 0)
    cols_i = lax.broadcasted_iota(jnp.int32, (CHUNK, CHUNK), 1)
    causal = cols_i <= rows_i
    for g in range(N_GROUPS):
        wsg = jnp.where(causal, ws_ref[g], 0.0).astype(BF16)
        bias = bst_ref[:, g:g + 1]
        for n in range(ts // CHUNK):
            r0, c0 = n * CHUNK, g * gw
            mixed = _dot(wsg, vb[r0:r0 + CHUNK, c0:c0 + gw]) + bias
            ya_ref[r0:r0 + CHUNK, c0:c0 + gw] = u[r0:r0 + CHUNK, c0:c0 + gw] * mixed
    merged = gated(ya_ref[...], wa_ref, 0)

    z = gc * xb
    zext_ref[HALO:HALO + ts, :] = z
    wconv = wconv_ref[...]
    conv = wconv[0:1] * zext_ref[HALO - 2:HALO - 2 + ts, :]
    conv = conv + wconv[1:2] * zext_ref[HALO - 1:HALO - 1 + ts, :]
    conv = conv + wconv[2:3] * z
    merged = merged + gated(gb * conv, wb_ref, 1)

    pext_ref[HALO:HALO + ts, :] = p
    pos = j * ts + lax.broadcasted_iota(jnp.int32, (ts, gw), 0)
    d_groups = []
    for g, w in enumerate(POOL_WINDOWS):
        c0 = g * gw
        s = pext_ref[HALO:HALO + ts, c0:c0 + gw]
        for k in range(1, w):
            s = s + pext_ref[HALO - k:HALO - k + ts, c0:c0 + gw]
        cnt = jnp.minimum(pos + 1, w).astype(F32)
        d_groups.append(s / cnt - p[:, c0:c0 + gw])
    merged = merged + gated(_branch_c_tail(d_groups, wpool_ref, pscale_ref), wc_ref, 2)

    out = _dot(merged.astype(BF16), wo_ref[...])
    o_ref[...] = x + _rms(out, ng[3:4])

    @pl.when(j == last_j)
    def _():
        conv_out_ref[...] = zext_ref[HALO + ts - (CONV_W - 1):HALO + ts, :]
        pool_out_ref[...] = pext_ref[HALO + ts - POOL_BUF:HALO + ts, :]

    zext_ref[0:HALO, :] = zext_ref[ts:ts + HALO, :]
    pext_ref[0:HALO, :] = pext_ref[ts:ts + HALO, :]


def _mixer_prompt(x, batch, norm_g, w_in, w_s, b_s_t, ln_g, ln_b, w_conv, w_pool, pool_scale,
                  w_a, w_b, w_c, w_o, layer):
    rows, d_model = x.shape
    seq = rows // batch
    d_br = w_a.shape[1]
    ts = min(ROW_TILE, seq)
    assert seq % ts == 0 and ts % CHUNK == 0 and ts >= HALO
    nj = seq // ts
    body = functools.partial(_mixer_prompt_body, ts=ts, d_model=d_model, d_br=d_br)

    def whole(a):
        nd = a.ndim - 1
        return _resident((None,) + a.shape[1:], lambda b, j, nd=nd: (layer,) + (0,) * nd)

    return pl.pallas_call(
        body,
        grid=(batch, nj),
        in_specs=[pl.BlockSpec((ts, d_model), lambda b, j: (b * nj + j, 0))]
        + [whole(a) for a in (norm_g, w_in, w_s, b_s_t, ln_g, ln_b, w_conv, w_pool, pool_scale,
                              w_a, w_b, w_c, w_o)],
        out_specs=[
            pl.BlockSpec((ts, d_model), lambda b, j: (b * nj + j, 0)),
            pl.BlockSpec((None, CONV_W - 1, d_br), lambda b, j: (b, 0, 0)),
            pl.BlockSpec((None, POOL_BUF, d_br), lambda b, j: (b, 0, 0)),
        ],
        out_shape=[
            jax.ShapeDtypeStruct((rows, d_model), F32),
            jax.ShapeDtypeStruct((batch, CONV_W - 1, d_br), F32),
            jax.ShapeDtypeStruct((batch, POOL_BUF, d_br), F32),
        ],
        scratch_shapes=[
            pltpu.VMEM((HALO + ts, d_br), F32),
            pltpu.VMEM((HALO + ts, d_br), F32),
            pltpu.VMEM((ts, d_br), F32),
        ],
        compiler_params=pltpu.CompilerParams(
            dimension_semantics=("arbitrary", "arbitrary"), vmem_limit_bytes=VMEM_LIMIT_BYTES),
        name=f"mixer_prompt_l{layer}",
    )(x, norm_g, w_in, w_s, b_s_t, ln_g, ln_b, w_conv, w_pool, pool_scale, w_a, w_b, w_c, w_o)


def _mixer_sample_body(ws_ref, bs_ref, x_ref, conv_ref, pool_ref, ng_ref, win_ref, lng_ref, lnb_ref,
                       wconv_ref, wpool_ref, pscale_ref, wa_ref, wb_ref, wc_ref, wo_ref,
                       o_ref, v_out_ref, conv_out_ref, pool_out_ref,
                       *, layer, nb, steps, d_model, d_br, start_pos):
    gw = d_br // N_GROUPS
    x = x_ref[...]
    ng = ng_ref[...]
    h = _rms(x, ng[2:3]).astype(BF16)

    def blk(a, i):
        return a[i * nb:(i + 1) * nb]

    u = jax.nn.gelu(_project(h, win_ref, 0, d_br))
    v = _layer_norm(jax.nn.gelu(_project(h, win_ref, d_br, 2 * d_br)), lng_ref[...], lnb_ref[...])
    v_out_ref[...] = v
    ya_rows = []
    for t in range(steps):
        cols = []
        for g in range(N_GROUPS):
            c0 = g * gw
            mixed = None
            for s in range(t + 1):
                term = ws_ref[layer, g * steps * steps + t * steps + s] * blk(v, s)[:, c0:c0 + gw]
                mixed = term if mixed is None else mixed + term
            cols.append(mixed + bs_ref[layer, g * steps + t])
        ya_rows.append(blk(u, t) * jnp.concatenate(cols, axis=-1))
    ya = jnp.concatenate(ya_rows, axis=0)

    z = _project(h, win_ref, 4 * d_br, 5 * d_br) * _project(h, win_ref, 2 * d_br, 3 * d_br)
    z_ext = [conv_ref[k] for k in range(CONV_W - 1)] + [blk(z, i) for i in range(steps)]
    wconv = wconv_ref[...]
    conv_rows = []
    for i in range(steps):
        c = wconv[0:1] * z_ext[i]
        for k in range(1, CONV_W):
            c = c + wconv[k:k + 1] * z_ext[i + k]
        conv_rows.append(c)
    yb = _project(h, win_ref, 3 * d_br, 4 * d_br) * jnp.concatenate(conv_rows, axis=0)
    for k in range(CONV_W - 1):
        conv_out_ref[k] = z_ext[steps + k]

    p = _project(h, win_ref, 5 * d_br, 6 * d_br)
    p_ext = [pool_ref[k] for k in range(POOL_BUF)] + [blk(p, i) for i in range(steps)]
    d_groups = []
    for g, w in enumerate(POOL_WINDOWS):
        c0 = g * gw
        rows_g = []
        for i in range(steps):
            s = p_ext[POOL_BUF + i][:, c0:c0 + gw]
            for k in range(1, w):
                s = s + p_ext[POOL_BUF + i - k][:, c0:c0 + gw]
            cnt = float(min(start_pos + i + 1, w))
            rows_g.append(s / cnt - p_ext[POOL_BUF + i][:, c0:c0 + gw])
        d_groups.append(jnp.concatenate(rows_g, axis=0))
    yc = _branch_c_tail(d_groups, wpool_ref, pscale_ref)
    for k in range(POOL_BUF):
        pool_out_ref[k] = p_ext[steps + k]

    out = _merge_and_project(h, win_ref, 6 * d_br, d_model, ya, yb, yc, wa_ref, wb_ref, wc_ref, wo_ref)
    o_ref[...] = x + _rms(out, ng[3:4])


def _mixer_sample(x, conv_state, pool_state, steps, norm_g, w_in, ws_small, bs_small, ln_g, ln_b,
                  w_conv, w_pool, pool_scale, w_a, w_b, w_c, w_o, layer):
    rows, d_model = x.shape
    nb = rows // steps
    d_br = w_a.shape[1]
    assert steps <= CHUNK and nb % SUBLANES == 0
    body = functools.partial(_mixer_sample_body, layer=layer, nb=nb, steps=steps, d_model=d_model,
                             d_br=d_br, start_pos=PAST_LEN)

    def whole(a):
        nd = a.ndim - 1
        return _resident((None,) + a.shape[1:], lambda i, nd=nd: (layer,) + (0,) * nd)

    smem = pl.BlockSpec(memory_space=pltpu.SMEM)

    return pl.pallas_call(
        body,
        grid=(1,),
        in_specs=[smem, smem,
                  pl.BlockSpec((rows, d_model), lambda i: (0, 0)),
                  whole(conv_state), whole(pool_state)]
        + [whole(a) for a in (norm_g, w_in, ln_g, ln_b, w_conv, w_pool, pool_scale, w_a, w_b, w_c, w_o)],
        out_specs=[
            pl.BlockSpec((rows, d_model), lambda i: (0, 0)),
            pl.BlockSpec((rows, d_br), lambda i: (0, 0)),
            pl.BlockSpec((CONV_W - 1, nb, d_br), lambda i: (0, 0, 0)),
            pl.BlockSpec((POOL_BUF, nb, d_br), lambda i: (0, 0, 0)),
        ],
        out_shape=[
            jax.ShapeDtypeStruct((rows, d_model), F32),
            jax.ShapeDtypeStruct((rows, d_br), F32),
            jax.ShapeDtypeStruct((CONV_W - 1, nb, d_br), F32),
            jax.ShapeDtypeStruct((POOL_BUF, nb, d_br), F32),
        ],
        compiler_params=pltpu.CompilerParams(
            dimension_semantics=("arbitrary",), vmem_limit_bytes=VMEM_LIMIT_BYTES),
        name=f"mixer_sample_l{layer}",
    )(ws_small, bs_small, x, conv_state, pool_state, norm_g, w_in, ln_g, ln_b, w_conv, w_pool,
      pool_scale, w_a, w_b, w_c, w_o)


def kernel(x_prompt, x_sample, state_conv, state_pool, norm_g, w_ffn_gu, w_ffn_down, w_in, w_s, b_s,
           ln_g, ln_b, w_conv, w_pool, pool_scale, w_a_out, w_b_out, w_c_out, w_o):
    depth = w_in.shape[0]
    batch, seq, d_model = x_prompt.shape
    nb, steps, _ = x_sample.shape

    wgu, wdown, win = w_ffn_gu.astype(BF16), w_ffn_down.astype(BF16), w_in.astype(BF16)
    wpool = w_pool.astype(BF16)
    wa, wb, wc, wo = (w.astype(BF16) for w in (w_a_out, w_b_out, w_c_out, w_o))
    lng, lnb, pscale = (a[:, None, :] for a in (ln_g, ln_b, pool_scale))
    b_s_t = jnp.swapaxes(b_s, 1, 2)
    ws_small = w_s[:, :, :steps, :steps].reshape(depth, -1)
    bs_small = b_s[:, :, :steps].reshape(depth, -1)

    xp = x_prompt.reshape(batch * seq, d_model)
    xs = jnp.swapaxes(x_sample, 0, 1).reshape(steps * nb, d_model)
    conv_s = jnp.swapaxes(state_conv, 1, 2)
    pool_s = jnp.swapaxes(state_pool, 1, 2)

    conv_p_out, pool_p_out, conv_s_out, pool_s_out, v_s_out = [], [], [], [], []
    for l in range(depth):
        xp = _ffn(xp, norm_g, wgu, wdown, l, 0)
        xs = _ffn(xs, norm_g, wgu, wdown, l, 0)
        xp, cp, pp = _mixer_prompt(xp, batch, norm_g, win, w_s, b_s_t, lng, lnb, w_conv, wpool, pscale,
                                   wa, wb, wc, wo, l)
        xs, vs, cs, ps = _mixer_sample(xs, conv_s, pool_s, steps, norm_g, win, ws_small, bs_small, lng,
                                       lnb, w_conv, wpool, pscale, wa, wb, wc, wo, l)
        xp = _ffn(xp, norm_g, wgu, wdown, l, 1)
        xs = _ffn(xs, norm_g, wgu, wdown, l, 1)
        conv_p_out.append(cp)
        pool_p_out.append(pp)
        conv_s_out.append(jnp.swapaxes(cs, 0, 1))
        pool_s_out.append(jnp.swapaxes(ps, 0, 1))
        v_s_out.append(jnp.swapaxes(vs.reshape(steps, nb, -1), 0, 1))

    y_prompt = xp.reshape(batch, seq, d_model)
    y_sample = jnp.swapaxes(xs.reshape(steps, nb, d_model), 0, 1)
    return (y_prompt, y_sample, jnp.stack(conv_p_out), jnp.stack(pool_p_out),
            jnp.stack(conv_s_out), jnp.stack(pool_s_out), jnp.stack(v_s_out))
```

```python
import functools

import jax
import jax.numpy as jnp
from jax import lax
from jax.experimental import pallas as pl
from jax.experimental.pallas import tpu as pltpu

EPS = 1e-6
PAST_LEN = 16384
CHUNK = 128
N_GROUPS = 4
POOL_WINDOWS = (2, 4, 8, 16)
POOL_BUF = max(POOL_WINDOWS) - 1
CONV_W = 3
N_BRANCH = 3

SUBLANES = 8
LANES = 128
HALO = 16
FFN_COL_CHUNK = 512
ROW_TILE = 512
FFN_ROW_TILE = 1024
MXU_COLS = 256
VMEM_LIMIT_BYTES = 56 * 1024 * 1024

F32 = jnp.float32
BF16 = jnp.bfloat16


def _rms(x, g):
    return x * lax.rsqrt(jnp.mean(x * x, axis=-1, keepdims=True) + EPS) * g


def _layer_norm(x, g, b):
    xc = x - jnp.mean(x, axis=-1, keepdims=True)
    return xc * lax.rsqrt(jnp.mean(xc * xc, axis=-1, keepdims=True) + EPS) * g + b


def _dot(a, b):
    return jnp.dot(a, b, preferred_element_type=F32)


def _resident(block_shape, index_map):
    return pl.BlockSpec(block_shape, index_map, pipeline_mode=pl.Buffered(1))


def _ffn_body(x_ref, ng_ref, wgu_ref, wdown_ref, o_ref, *, d_ff, pre, post, sub_rows):
    ng = ng_ref[...]
    for r0 in range(0, x_ref.shape[0], sub_rows):
        x = x_ref[r0:r0 + sub_rows, :]
        h = _rms(x, ng[pre:pre + 1]).astype(BF16)
        acc = None
        for c0 in range(0, d_ff, FFN_COL_CHUNK):
            c1 = min(c0 + FFN_COL_CHUNK, d_ff)
            g = _dot(h, wgu_ref[:, c0:c1])
            u = _dot(h, wgu_ref[:, d_ff + c0:d_ff + c1])
            a = (g * jax.nn.sigmoid(g) * u).astype(BF16)
            part = _dot(a, wdown_ref[c0:c1, :])
            acc = part if acc is None else acc + part
        o_ref[r0:r0 + sub_rows, :] = x + 0.5 * _rms(acc, ng[post:post + 1])


def _ffn(x, norm_g, w_gu, w_down, layer, which):
    rows, d = x.shape
    d_ff = w_down.shape[2]
    tm = min(FFN_ROW_TILE, rows)
    sub_rows = min(ROW_TILE, tm)
    assert rows % tm == 0 and tm % sub_rows == 0
    body = functools.partial(_ffn_body, d_ff=d_ff, pre=4 * which, post=4 * which + 1,
                             sub_rows=sub_rows)
    return pl.pallas_call(
        body,
        grid=(rows // tm,),
        in_specs=[
            pl.BlockSpec((tm, d), lambda i: (i, 0)),
            _resident((None,) + norm_g.shape[1:], lambda i: (layer, 0, 0)),
            _resident((None, None) + w_gu.shape[2:], lambda i: (layer, which, 0, 0)),
            _resident((None, None) + w_down.shape[2:], lambda i: (layer, which, 0, 0)),
        ],
        out_specs=pl.BlockSpec((tm, d), lambda i: (i, 0)),
        out_shape=jax.ShapeDtypeStruct((rows, d), F32),
        compiler_params=pltpu.CompilerParams(
            dimension_semantics=("arbitrary",), vmem_limit_bytes=VMEM_LIMIT_BYTES),
        name=f"ffn{which}_l{layer}_r{rows}",
    )(x, norm_g, w_gu, w_down)


def _project(h, win_ref, lo, hi):
    return _dot(h, win_ref[:, lo:hi])


def _branch_c_tail(d_groups, wpool_ref, pscale_ref):
    outs = [_dot(d_groups[g].astype(BF16), wpool_ref[g]) for g in range(N_GROUPS)]
    return jnp.concatenate(outs, axis=-1) * pscale_ref[...]


def _merge_and_project(h, win_ref, gate0, d_model, ya, yb, yc, wa_ref, wb_ref, wc_ref, wo_ref):
    merged = None
    for k, (y, w_ref) in enumerate(((ya, wa_ref), (yb, wb_ref), (yc, wc_ref))):
        gate = jax.nn.sigmoid(_project(h, win_ref, gate0 + k * d_model, gate0 + (k + 1) * d_model))
        term = gate * _dot(y.astype(BF16), w_ref[...])
        merged = term if merged is None else merged + term
    return _dot(merged.astype(BF16), wo_ref[...])


def _mixer_prompt_body(x_ref, ng_ref, win_ref, ws_ref, bst_ref, lng_ref, lnb_ref, wconv_ref,
                       wpool_ref, pscale_ref, wa_ref, wb_ref, wc_ref, wo_ref,
                       o_ref, conv_out_ref, pool_out_ref,
                       zext_ref, pext_ref, ya_ref, *, ts, d_model, d_br):
    j = pl.program_id(1)
    last_j = pl.num_programs(1) - 1
    gw = d_br // N_GROUPS

    @pl.when(j == 0)
    def _():
        zext_ref[0:HALO, :] = jnp.zeros((HALO, d_br), F32)
        pext_ref[0:HALO, :] = jnp.zeros((HALO, d_br), F32)

    x = x_ref[...]
    ng = ng_ref[...]
    h = _rms(x, ng[2:3]).astype(BF16)

    def slabs(lo, hi):
        return [functools.partial(_project, h, win_ref, c, c + MXU_COLS) for c in range(lo, hi, MXU_COLS)]

    def join(parts):
        return jnp.concatenate(parts, axis=-1)

    v_raw = _project(h, win_ref, d_br, 2 * d_br)
    u_raw = _project(h, win_ref, 0, d_br)
    u = jax.nn.gelu(u_raw)
    vb = _layer_norm(jax.nn.gelu(v_raw), lng_ref[...], lnb_ref[...]).astype(BF16)

    rows_i = lax.broadcasted_iota(jnp.int32, (CHUNK, CHUNK), 0)
    cols_i = lax.broadcasted_iota(jnp.int32, (CHUNK, CHUNK), 1)
    causal = cols_i <= rows_i

    def spatial(g, n):
        r0, c0 = n * CHUNK, g * gw
        wsg = jnp.where(causal, ws_ref[g], 0.0).astype(BF16)
        mixed = _dot(wsg, vb[r0:r0 + CHUNK, c0:c0 + gw]) + bst_ref[:, g:g + 1]
        ya_ref[r0:r0 + CHUNK, c0:c0 + gw] = u[r0:r0 + CHUNK, c0:c0 + gw] * mixed

    small = [functools.partial(spatial, g, n) for g in range(N_GROUPS) for n in range(ts // CHUNK)]
    big = slabs(2 * d_br, 6 * d_br + 2 * d_model)
    done = []
    for k, slab in enumerate(big):
        done.append(slab())
        if k < len(small):
            small[k]()
    for rest in small[len(big):]:
        rest()
    per = d_br // MXU_COLS
    xb, gb, gc, p = (join(done[i * per:(i + 1) * per]) for i in range(4))
    per_gate = d_model // MXU_COLS
    gate_a = jax.nn.sigmoid(join(done[4 * per:4 * per + per_gate]))
    gate_b = jax.nn.sigmoid(join(done[4 * per + per_gate:]))
    merged = gate_a * _dot(ya_ref[...].astype(BF16), wa_ref[...])

    z = gc * xb
    zext_ref[HALO:HALO + ts, :] = z
    wconv = wconv_ref[...]
    conv = wconv[0:1] * zext_ref[HALO - 2:HALO - 2 + ts, :]
    conv = conv + wconv[1:2] * zext_ref[HALO - 1:HALO - 1 + ts, :]
    conv = conv + wconv[2:3] * z
    yb = (gb * conv).astype(BF16)

    pext_ref[HALO:HALO + ts, :] = p
    pos = j * ts + lax.broadcasted_iota(jnp.int32, (ts, gw), 0)
    gate_c_parts, yc_parts = [], []
    for g, (w, slab) in enumerate(zip(POOL_WINDOWS, slabs(6 * d_br + 2 * d_model, 6 * d_br + 3 * d_model))):
        c0 = g * gw
        s = pext_ref[HALO:HALO + ts, c0:c0 + gw]
        for k in range(1, w):
            s = s + pext_ref[HALO - k:HALO - k + ts, c0:c0 + gw]
        cnt = jnp.minimum(pos + 1, w).astype(F32)
        d = s / cnt - p[:, c0:c0 + gw]
        gate_c_parts.append(slab())
        yc_parts.append(_dot(d.astype(BF16), wpool_ref[g]))
    yc = join(yc_parts) * pscale_ref[...]

    merged = merged + gate_b * _dot(yb, wb_ref[...])
    merged = merged + jax.nn.sigmoid(join(gate_c_parts)) * _dot(yc.astype(BF16), wc_ref[...])

    out = _dot(merged.astype(BF16), wo_ref[...])
    o_ref[...] = x + _rms(out, ng[3:4])

    @pl.when(j == last_j)
    def _():
        conv_out_ref[...] = zext_ref[HALO + ts - (CONV_W - 1):HALO + ts, :]
        pool_out_ref[...] = pext_ref[HALO + ts - POOL_BUF:HALO + ts, :]

    zext_ref[0:HALO, :] = zext_ref[ts:ts + HALO, :]
    pext_ref[0:HALO, :] = pext_ref[ts:ts + HALO, :]


def _mixer_prompt(x, batch, norm_g, w_in, w_s, b_s_t, ln_g, ln_b, w_conv, w_pool, pool_scale,
                  w_a, w_b, w_c, w_o, layer):
    rows, d_model = x.shape
    seq = rows // batch
    d_br = w_a.shape[1]
    ts = min(ROW_TILE, seq)
    assert seq % ts == 0 and ts % CHUNK == 0 and ts >= HALO
    assert d_br % MXU_COLS == 0 and d_model % MXU_COLS == 0 and d_model // MXU_COLS == N_GROUPS
    nj = seq // ts
    body = functools.partial(_mixer_prompt_body, ts=ts, d_model=d_model, d_br=d_br)

    def whole(a):
        nd = a.ndim - 1
        return _resident((None,) + a.shape[1:], lambda b, j, nd=nd: (layer,) + (0,) * nd)

    return pl.pallas_call(
        body,
        grid=(batch, nj),
        in_specs=[pl.BlockSpec((ts, d_model), lambda b, j: (b * nj + j, 0))]
        + [whole(a) for a in (norm_g, w_in, w_s, b_s_t, ln_g, ln_b, w_conv, w_pool, pool_scale,
                              w_a, w_b, w_c, w_o)],
        out_specs=[
            pl.BlockSpec((ts, d_model), lambda b, j: (b * nj + j, 0)),
            pl.BlockSpec((None, CONV_W - 1, d_br), lambda b, j: (b, 0, 0)),
            pl.BlockSpec((None, POOL_BUF, d_br), lambda b, j: (b, 0, 0)),
        ],
        out_shape=[
            jax.ShapeDtypeStruct((rows, d_model), F32),
            jax.ShapeDtypeStruct((batch, CONV_W - 1, d_br), F32),
            jax.ShapeDtypeStruct((batch, POOL_BUF, d_br), F32),
        ],
        scratch_shapes=[
            pltpu.VMEM((HALO + ts, d_br), F32),
            pltpu.VMEM((HALO + ts, d_br), F32),
            pltpu.VMEM((ts, d_br), F32),
        ],
        compiler_params=pltpu.CompilerParams(
            dimension_semantics=("arbitrary", "arbitrary"), vmem_limit_bytes=VMEM_LIMIT_BYTES),
        name=f"mixer_prompt_l{layer}",
    )(x, norm_g, w_in, w_s, b_s_t, ln_g, ln_b, w_conv, w_pool, pool_scale, w_a, w_b, w_c, w_o)


def _mixer_sample_body(ws_ref, bs_ref, x_ref, conv_ref, pool_ref, ng_ref, win_ref, lng_ref, lnb_ref,
                       wconv_ref, wpool_ref, pscale_ref, wa_ref, wb_ref, wc_ref, wo_ref,
                       o_ref, v_out_ref, conv_out_ref, pool_out_ref,
                       *, layer, nb, steps, d_model, d_br, start_pos):
    gw = d_br // N_GROUPS
    x = x_ref[...]
    ng = ng_ref[...]
    h = _rms(x, ng[2:3]).astype(BF16)

    def blk(a, i):
        return a[i * nb:(i + 1) * nb]

    u = jax.nn.gelu(_project(h, win_ref, 0, d_br))
    v = _layer_norm(jax.nn.gelu(_project(h, win_ref, d_br, 2 * d_br)), lng_ref[...], lnb_ref[...])
    v_out_ref[...] = v
    ya_rows = []
    for t in range(steps):
        cols = []
        for g in range(N_GROUPS):
            c0 = g * gw
            mixed = None
            for s in range(t + 1):
                term = ws_ref[layer, g * steps * steps + t * steps + s] * blk(v, s)[:, c0:c0 + gw]
                mixed = term if mixed is None else mixed + term
            cols.append(mixed + bs_ref[layer, g * steps + t])
        ya_rows.append(blk(u, t) * jnp.concatenate(cols, axis=-1))
    ya = jnp.concatenate(ya_rows, axis=0)

    z = _project(h, win_ref, 4 * d_br, 5 * d_br) * _project(h, win_ref, 2 * d_br, 3 * d_br)
    z_ext = [conv_ref[k] for k in range(CONV_W - 1)] + [blk(z, i) for i in range(steps)]
    wconv = wconv_ref[...]
    conv_rows = []
    for i in range(steps):
        c = wconv[0:1] * z_ext[i]
        for k in range(1, CONV_W):
            c = c + wconv[k:k + 1] * z_ext[i + k]
        conv_rows.append(c)
    yb = _project(h, win_ref, 3 * d_br, 4 * d_br) * jnp.concatenate(conv_rows, axis=0)
    for k in range(CONV_W - 1):
        conv_out_ref[k] = z_ext[steps + k]

    p = _project(h, win_ref, 5 * d_br, 6 * d_br)
    p_ext = [pool_ref[k] for k in range(POOL_BUF)] + [blk(p, i) for i in range(steps)]
    d_groups = []
    for g, w in enumerate(POOL_WINDOWS):
        c0 = g * gw
        rows_g = []
        for i in range(steps):
            s = p_ext[POOL_BUF + i][:, c0:c0 + gw]
            for k in range(1, w):
                s = s + p_ext[POOL_BUF + i - k][:, c0:c0 + gw]
            cnt = float(min(start_pos + i + 1, w))
            rows_g.append(s / cnt - p_ext[POOL_BUF + i][:, c0:c0 + gw])
        d_groups.append(jnp.concatenate(rows_g, axis=0))
    yc = _branch_c_tail(d_groups, wpool_ref, pscale_ref)
    for k in range(POOL_BUF):
        pool_out_ref[k] = p_ext[steps + k]

    out = _merge_and_project(h, win_ref, 6 * d_br, d_model, ya, yb, yc, wa_ref, wb_ref, wc_ref, wo_ref)
    o_ref[...] = x + _rms(out, ng[3:4])


def _mixer_sample(x, conv_state, pool_state, steps, norm_g, w_in, ws_small, bs_small, ln_g, ln_b,
                  w_conv, w_pool, pool_scale, w_a, w_b, w_c, w_o, layer):
    rows, d_model = x.shape
    nb = rows // steps
    d_br = w_a.shape[1]
    assert steps <= CHUNK and nb % SUBLANES == 0
    body = functools.partial(_mixer_sample_body, layer=layer, nb=nb, steps=steps, d_model=d_model,
                             d_br=d_br, start_pos=PAST_LEN)

    def whole(a):
        nd = a.ndim - 1
        return _resident((None,) + a.shape[1:], lambda i, nd=nd: (layer,) + (0,) * nd)

    smem = pl.BlockSpec(memory_space=pltpu.SMEM)

    return pl.pallas_call(
        body,
        grid=(1,),
        in_specs=[smem, smem,
                  pl.BlockSpec((rows, d_model), lambda i: (0, 0)),
                  whole(conv_state), whole(pool_state)]
        + [whole(a) for a in (norm_g, w_in, ln_g, ln_b, w_conv, w_pool, pool_scale, w_a, w_b, w_c, w_o)],
        out_specs=[
            pl.BlockSpec((rows, d_model), lambda i: (0, 0)),
            pl.BlockSpec((rows, d_br), lambda i: (0, 0)),
            pl.BlockSpec((CONV_W - 1, nb, d_br), lambda i: (0, 0, 0)),
            pl.BlockSpec((POOL_BUF, nb, d_br), lambda i: (0, 0, 0)),
        ],
        out_shape=[
            jax.ShapeDtypeStruct((rows, d_model), F32),
            jax.ShapeDtypeStruct((rows, d_br), F32),
            jax.ShapeDtypeStruct((CONV_W - 1, nb, d_br), F32),
            jax.ShapeDtypeStruct((POOL_BUF, nb, d_br), F32),
        ],
        compiler_params=pltpu.CompilerParams(
            dimension_semantics=("arbitrary",), vmem_limit_bytes=VMEM_LIMIT_BYTES),
        name=f"mixer_sample_l{layer}",
    )(ws_small, bs_small, x, conv_state, pool_state, norm_g, w_in, ln_g, ln_b, w_conv, w_pool,
      pool_scale, w_a, w_b, w_c, w_o)


def kernel(x_prompt, x_sample, state_conv, state_pool, norm_g, w_ffn_gu, w_ffn_down, w_in, w_s, b_s,
           ln_g, ln_b, w_conv, w_pool, pool_scale, w_a_out, w_b_out, w_c_out, w_o):
    depth = w_in.shape[0]
    batch, seq, d_model = x_prompt.shape
    nb, steps, _ = x_sample.shape

    wgu, wdown, win = w_ffn_gu.astype(BF16), w_ffn_down.astype(BF16), w_in.astype(BF16)
    wpool = w_pool.astype(BF16)
    wa, wb, wc, wo = (w.astype(BF16) for w in (w_a_out, w_b_out, w_c_out, w_o))
    lng, lnb, pscale = (a[:, None, :] for a in (ln_g, ln_b, pool_scale))
    b_s_t = jnp.swapaxes(b_s, 1, 2)
    ws_small = w_s[:, :, :steps, :steps].reshape(depth, -1)
    bs_small = b_s[:, :, :steps].reshape(depth, -1)

    xp = x_prompt.reshape(batch * seq, d_model)
    xs = jnp.swapaxes(x_sample, 0, 1).reshape(steps * nb, d_model)
    conv_s = jnp.swapaxes(state_conv, 1, 2)
    pool_s = jnp.swapaxes(state_pool, 1, 2)

    conv_p_out, pool_p_out, conv_s_out, pool_s_out, v_s_out = [], [], [], [], []
    for l in range(depth):
        xp = _ffn(xp, norm_g, wgu, wdown, l, 0)
        xs = _ffn(xs, norm_g, wgu, wdown, l, 0)
        xp, cp, pp = _mixer_prompt(xp, batch, norm_g, win, w_s, b_s_t, lng, lnb, w_conv, wpool, pscale,
                                   wa, wb, wc, wo, l)
        xs, vs, cs, ps = _mixer_sample(xs, conv_s, pool_s, steps, norm_g, win, ws_small, bs_small, lng,
                                       lnb, w_conv, wpool, pscale, wa, wb, wc, wo, l)
        xp = _ffn(xp, norm_g, wgu, wdown, l, 1)
        xs = _ffn(xs, norm_g, wgu, wdown, l, 1)
        conv_p_out.append(cp)
        pool_p_out.append(pp)
        conv_s_out.append(jnp.swapaxes(cs, 0, 1))
        pool_s_out.append(jnp.swapaxes(ps, 0, 1))
        v_s_out.append(jnp.swapaxes(vs.reshape(steps, nb, -1), 0, 1))

    y_prompt = xp.reshape(batch, seq, d_model)
    y_sample = jnp.swapaxes(xs.reshape(steps, nb, d_model), 0, 1)
    return (y_prompt, y_sample, jnp.stack(conv_p_out), jnp.stack(pool_p_out),
            jnp.stack(conv_s_out), jnp.stack(pool_s_out), jnp.stack(v_s_out))
```

```python
import functools

import jax
import jax.numpy as jnp
from jax import lax
from jax.experimental import pallas as pl
from jax.experimental.pallas import tpu as pltpu

EPS = 1e-6
PAST_LEN = 16384
CHUNK = 128
N_GROUPS = 4
POOL_WINDOWS = (2, 4, 8, 16)
POOL_BUF = max(POOL_WINDOWS) - 1
CONV_W = 3

SUBLANES = 8
BF16_ROWS = 16
HALO = 16
FFN_COL_CHUNK = 512
ROW_TILE = 512
MXU_COLS = 256
VMEM_LIMIT_BYTES = 60 * 1024 * 1024

F32 = jnp.float32
BF16 = jnp.bfloat16


def _rms(x, g):
    return x * lax.rsqrt(jnp.mean(x * x, axis=-1, keepdims=True) + EPS) * g


def _layer_norm(x, g, b):
    xc = x - jnp.mean(x, axis=-1, keepdims=True)
    return xc * lax.rsqrt(jnp.mean(xc * xc, axis=-1, keepdims=True) + EPS) * g + b


def _dot(a, b):
    return jnp.dot(a, b, preferred_element_type=F32)


def _resident(block_shape, index_map):
    return pl.BlockSpec(block_shape, index_map, pipeline_mode=pl.Buffered(1))


def _whole(a, lead=()):
    nd = a.ndim - len(lead)
    return _resident((None,) * len(lead) + a.shape[len(lead):], lambda t: tuple(lead) + (0,) * nd)


class _Cast:
    def __init__(self, w, lead, n_prompt):
        rows, cols = w.shape[-2:]
        share = 1
        while n_prompt % share or rows % (n_prompt // share) or (rows // (n_prompt // share)) % BF16_ROWS:
            share *= 2
            assert share <= n_prompt
        n_blocks = n_prompt // share
        blk = rows // n_blocks

        def block(t):
            return jnp.minimum(t // share, n_blocks - 1)

        self.operand = w
        self.in_spec = pl.BlockSpec((None,) * len(lead) + (blk, cols), lambda t: tuple(lead) + (block(t), 0))
        self.out_spec = pl.BlockSpec((blk, cols), lambda t: (block(t), 0))
        self.out_shape = jax.ShapeDtypeStruct((rows, cols), BF16)


def _run_casts(cast_in, cast_out):
    for src, dst in zip(cast_in, cast_out):
        dst[...] = src[...].astype(BF16)


def _ffn_body(*refs, d_ff, pre, post, two_in, two_out, n_cast, n_prompt):
    it = iter(refs)
    xp_ref = next(it)
    xs_ref = next(it) if two_in else None
    ng_ref, wgu_ref, wdown_ref = next(it), next(it), next(it)
    cast_in = [next(it) for _ in range(n_cast)]
    op_ref = next(it)
    os_ref = next(it) if two_out else None
    cast_out = [next(it) for _ in range(n_cast)]

    t = pl.program_id(0)
    x = xp_ref[...]
    if two_in:
        x = jnp.where(t < n_prompt, x, xs_ref[...])
    ng = ng_ref[...]
    h = _rms(x, ng[pre:pre + 1]).astype(BF16)
    acc = None
    for c0 in range(0, d_ff, FFN_COL_CHUNK):
        c1 = min(c0 + FFN_COL_CHUNK, d_ff)
        g = _dot(h, wgu_ref[:, c0:c1])
        u = _dot(h, wgu_ref[:, d_ff + c0:d_ff + c1])
        a = (g * jax.nn.sigmoid(g) * u).astype(BF16)
        part = _dot(a, wdown_ref[c0:c1, :])
        acc = part if acc is None else acc + part
    y = x + 0.5 * _rms(acc, ng[post:post + 1])
    if two_out:
        @pl.when(t < n_prompt)
        def _():
            op_ref[...] = y

        @pl.when(t == n_prompt)
        def _():
            os_ref[...] = y
    else:
        op_ref[...] = y
    _run_casts(cast_in, cast_out)


def _ffn(xs_in, norm_g, wgu, wdown, casts, layer, which, n_prompt, two_out):
    two_in = len(xs_in) == 2
    d = xs_in[0].shape[1]
    d_ff = wdown.shape[0]
    tile = pl.BlockSpec((ROW_TILE, d), lambda t: (t, 0))
    prompt_tile = pl.BlockSpec((ROW_TILE, d), lambda t: (jnp.minimum(t, n_prompt - 1), 0))
    sample_tile = pl.BlockSpec((ROW_TILE, d), lambda t: (0, 0))
    rows = (n_prompt + 1) * ROW_TILE
    body = functools.partial(_ffn_body, d_ff=d_ff, pre=4 * which, post=4 * which + 1, two_in=two_in,
                             two_out=two_out, n_cast=len(casts), n_prompt=n_prompt)
    if two_out:
        x_out_specs = [prompt_tile, sample_tile]
        x_out_shapes = [jax.ShapeDtypeStruct((n_prompt * ROW_TILE, d), F32),
                        jax.ShapeDtypeStruct((ROW_TILE, d), F32)]
    else:
        x_out_specs = [tile]
        x_out_shapes = [jax.ShapeDtypeStruct((rows, d), F32)]
    return pl.pallas_call(
        body,
        grid=(n_prompt + 1,),
        in_specs=([prompt_tile, sample_tile] if two_in else [tile])
        + [_whole(norm_g, (layer,)), _whole(wgu), _whole(wdown)] + [c.in_spec for c in casts],
        out_specs=x_out_specs + [c.out_spec for c in casts],
        out_shape=x_out_shapes + [c.out_shape for c in casts],
        compiler_params=pltpu.CompilerParams(
            dimension_semantics=("arbitrary",), vmem_limit_bytes=VMEM_LIMIT_BYTES),
        name=f"ffn{which}_l{layer}",
    )(*xs_in, norm_g, wgu, wdown, *[c.operand for c in casts])


def _project(h, win_ref, lo, hi):
    return _dot(h, win_ref[:, lo:hi])


def _join(parts):
    return jnp.concatenate(parts, axis=-1)


def _mixer_prompt_tile(j, is_last_j, x_ref, ng_ref, win_ref, ws_ref, bst_ref, lng_ref, lnb_ref, wconv_ref,
                       wpool_ref, pscale_ref, wa_ref, wb_ref, wc_ref, wo_ref,
                       o_ref, conv_out_ref, pool_out_ref, zext_ref, pext_ref, ya_ref, *, ts, d_model, d_br):
    gw = d_br // N_GROUPS

    @pl.when(j == 0)
    def _():
        zext_ref[0:HALO, :] = jnp.zeros((HALO, d_br), F32)
        pext_ref[0:HALO, :] = jnp.zeros((HALO, d_br), F32)

    x = x_ref[...]
    ng = ng_ref[...]
    h = _rms(x, ng[2:3]).astype(BF16)

    def slabs(lo, hi):
        return [functools.partial(_project, h, win_ref, c, c + MXU_COLS) for c in range(lo, hi, MXU_COLS)]

    v_raw = _project(h, win_ref, d_br, 2 * d_br)
    u_raw = _project(h, win_ref, 0, d_br)
    u = jax.nn.gelu(u_raw)
    vb = _layer_norm(jax.nn.gelu(v_raw), lng_ref[...], lnb_ref[...]).astype(BF16)

    rows_i = lax.broadcasted_iota(jnp.int32, (CHUNK, CHUNK), 0)
    cols_i = lax.broadcasted_iota(jnp.int32, (CHUNK, CHUNK), 1)
    causal = cols_i <= rows_i

    def spatial(g, n):
        r0, c0 = n * CHUNK, g * gw
        wsg = jnp.where(causal, ws_ref[g], 0.0).astype(BF16)
        mixed = _dot(wsg, vb[r0:r0 + CHUNK, c0:c0 + gw]) + bst_ref[:, g:g + 1]
        ya_ref[r0:r0 + CHUNK, c0:c0 + gw] = u[r0:r0 + CHUNK, c0:c0 + gw] * mixed

    small = [functools.partial(spatial, g, n) for g in range(N_GROUPS) for n in range(ts // CHUNK)]
    big = slabs(2 * d_br, 6 * d_br + 2 * d_model)
    done = []
    for k, slab in enumerate(big):
        done.append(slab())
        if k < len(small):
            small[k]()
    for rest in small[len(big):]:
        rest()
    per = d_br // MXU_COLS
    xb, gb, gc, p = (_join(done[i * per:(i + 1) * per]) for i in range(4))
    per_gate = d_model // MXU_COLS
    gate_a = jax.nn.sigmoid(_join(done[4 * per:4 * per + per_gate]))
    gate_b = jax.nn.sigmoid(_join(done[4 * per + per_gate:]))
    merged = gate_a * _dot(ya_ref[...].astype(BF16), wa_ref[...])

    z = gc * xb
    zext_ref[HALO:HALO + ts, :] = z
    wconv = wconv_ref[...]
    conv = wconv[0:1] * zext_ref[HALO - 2:HALO - 2 + ts, :]
    conv = conv + wconv[1:2] * zext_ref[HALO - 1:HALO - 1 + ts, :]
    conv = conv + wconv[2:3] * z
    yb = (gb * conv).astype(BF16)

    pext_ref[HALO:HALO + ts, :] = p
    pos = j * ts + lax.broadcasted_iota(jnp.int32, (ts, gw), 0)
    gate_c_parts, yc_parts = [], []
    for g, (w, slab) in enumerate(zip(POOL_WINDOWS, slabs(6 * d_br + 2 * d_model, 6 * d_br + 3 * d_model))):
        c0 = g * gw
        s = pext_ref[HALO:HALO + ts, c0:c0 + gw]
        for k in range(1, w):
            s = s + pext_ref[HALO - k:HALO - k + ts, c0:c0 + gw]
        cnt = jnp.minimum(pos + 1, w).astype(F32)
        d = s / cnt - p[:, c0:c0 + gw]
        gate_c_parts.append(slab())
        yc_parts.append(_dot(d.astype(BF16), wpool_ref[g]))
    yc = _join(yc_parts) * pscale_ref[...]

    merged = merged + gate_b * _dot(yb, wb_ref[...])
    merged = merged + jax.nn.sigmoid(_join(gate_c_parts)) * _dot(yc.astype(BF16), wc_ref[...])

    out = _dot(merged.astype(BF16), wo_ref[...])
    o_ref[...] = x + _rms(out, ng[3:4])

    @pl.when(is_last_j)
    def _():
        conv_out_ref[...] = zext_ref[HALO + ts - (CONV_W - 1):HALO + ts, :]
        pool_out_ref[...] = pext_ref[HALO + ts - POOL_BUF:HALO + ts, :]

    zext_ref[0:HALO, :] = zext_ref[ts:ts + HALO, :]
    pext_ref[0:HALO, :] = pext_ref[ts:ts + HALO, :]


def _mixer_sample_tile(ws_ref, bs_ref, x_ref, conv_ref, pool_ref, ng_ref, win_ref, lng_ref, lnb_ref,
                       wconv_ref, wpool_ref, pscale_ref, wa_ref, wb_ref, wc_ref, wo_ref,
                       o_ref, v_out_ref, z_out_ref, p_out_ref,
                       *, layer, nb, steps, d_model, d_br, start_pos):
    gw = d_br // N_GROUPS
    x = x_ref[...]
    ng = ng_ref[...]
    h = _rms(x, ng[2:3]).astype(BF16)

    def blk(a, i):
        return a[i * nb:(i + 1) * nb]

    u = jax.nn.gelu(_project(h, win_ref, 0, d_br))
    v = _layer_norm(jax.nn.gelu(_project(h, win_ref, d_br, 2 * d_br)), lng_ref[...], lnb_ref[...])
    v_out_ref[...] = v
    ya_rows = []
    for t in range(steps):
        cols = []
        for g in range(N_GROUPS):
            c0 = g * gw
            mixed = None
            for s in range(t + 1):
                term = ws_ref[layer, g * steps * steps + t * steps + s] * blk(v, s)[:, c0:c0 + gw]
                mixed = term if mixed is None else mixed + term
            cols.append(mixed + bs_ref[layer, g * steps + t])
        ya_rows.append(blk(u, t) * _join(cols))
    ya = jnp.concatenate(ya_rows, axis=0)

    z = _project(h, win_ref, 4 * d_br, 5 * d_br) * _project(h, win_ref, 2 * d_br, 3 * d_br)
    z_out_ref[...] = z
    z_ext = [conv_ref[k] for k in range(CONV_W - 1)] + [blk(z, i) for i in range(steps)]
    wconv = wconv_ref[...]
    conv_rows = []
    for i in range(steps):
        c = wconv[0:1] * z_ext[i]
        for k in range(1, CONV_W):
            c = c + wconv[k:k + 1] * z_ext[i + k]
        conv_rows.append(c)
    yb = _project(h, win_ref, 3 * d_br, 4 * d_br) * jnp.concatenate(conv_rows, axis=0)

    p = _project(h, win_ref, 5 * d_br, 6 * d_br)
    p_out_ref[...] = p
    p_ext = [pool_ref[k] for k in range(POOL_BUF)] + [blk(p, i) for i in range(steps)]
    d_groups = []
    for g, w in enumerate(POOL_WINDOWS):
        c0 = g * gw
        rows_g = []
        for i in range(steps):
            s = p_ext[POOL_BUF + i][:, c0:c0 + gw]
            for k in range(1, w):
                s = s + p_ext[POOL_BUF + i - k][:, c0:c0 + gw]
            cnt = float(min(start_pos + i + 1, w))
            rows_g.append(s / cnt - p_ext[POOL_BUF + i][:, c0:c0 + gw])
        d_groups.append(jnp.concatenate(rows_g, axis=0))
    yc = _join([_dot(d_groups[g].astype(BF16), wpool_ref[g]) for g in range(N_GROUPS)]) * pscale_ref[...]

    merged = None
    for k, (y, w_ref) in enumerate(((ya, wa_ref), (yb, wb_ref), (yc, wc_ref))):
        lo = 6 * d_br + k * d_model
        term = jax.nn.sigmoid(_project(h, win_ref, lo, lo + d_model)) * _dot(y.astype(BF16), w_ref[...])
        merged = term if merged is None else merged + term
    out = _dot(merged.astype(BF16), wo_ref[...])
    o_ref[...] = x + _rms(out, ng[3:4])


def _mixer_body(*refs, layer, n_prompt, nj, n_cast, ts, nb, steps, d_model, d_br):
    it = iter(refs)
    ws_sm, bs_sm, x_ref, conv_s_ref, pool_s_ref = (next(it) for _ in range(5))
    (ng_ref, win_ref, ws_ref, bst_ref, lng_ref, lnb_ref, wconv_ref, wpool_ref, pscale_ref,
     wa_ref, wb_ref, wc_ref, wo_ref) = (next(it) for _ in range(13))
    cast_in = [next(it) for _ in range(n_cast)]
    o_ref, conv_out_ref, pool_out_ref, v_out_ref, z_out_ref, p_out_ref = (next(it) for _ in range(6))
    cast_out = [next(it) for _ in range(n_cast)]
    zext_ref, pext_ref, ya_ref = next(it), next(it), next(it)

    t = pl.program_id(0)

    @pl.when(t < n_prompt)
    def _():
        j = lax.rem(t, nj)
        _mixer_prompt_tile(j, j == nj - 1, x_ref, ng_ref, win_ref, ws_ref, bst_ref, lng_ref, lnb_ref,
                           wconv_ref, wpool_ref, pscale_ref, wa_ref, wb_ref, wc_ref, wo_ref,
                           o_ref, conv_out_ref, pool_out_ref, zext_ref, pext_ref, ya_ref,
                           ts=ts, d_model=d_model, d_br=d_br)

    @pl.when(t == n_prompt)
    def _():
        _mixer_sample_tile(ws_sm, bs_sm, x_ref, conv_s_ref, pool_s_ref, ng_ref, win_ref, lng_ref, lnb_ref,
                           wconv_ref, wpool_ref, pscale_ref, wa_ref, wb_ref, wc_ref, wo_ref,
                           o_ref, v_out_ref, z_out_ref, p_out_ref,
                           layer=layer, nb=nb, steps=steps, d_model=d_model, d_br=d_br, start_pos=PAST_LEN)

    _run_casts(cast_in, cast_out)


def _mixer(x, batch, steps, conv_s, pool_s, ws_small, bs_small, norm_g, win, w_s, b_s_t, ln_g, ln_b,
           w_conv, wpool, pool_scale, wa, wb, wc, wo, casts, layer, n_prompt):
    rows, d_model = x.shape
    d_br = wa.shape[1]
    ts = ROW_TILE
    nj = n_prompt // batch
    nb = ROW_TILE // steps
    assert n_prompt % batch == 0 and ts % CHUNK == 0 and ts >= HALO
    assert steps <= CHUNK and nb * steps == ROW_TILE and nb % SUBLANES == 0
    assert d_br % MXU_COLS == 0 and d_model // MXU_COLS == N_GROUPS
    body = functools.partial(_mixer_body, layer=layer, n_prompt=n_prompt, nj=nj, n_cast=len(casts),
                             ts=ts, nb=nb, steps=steps, d_model=d_model, d_br=d_br)
    smem = pl.BlockSpec(memory_space=pltpu.SMEM)
    tile = pl.BlockSpec((ts, d_model), lambda t: (t, 0))
    seq_of = lambda t: jnp.minimum(t // nj, batch - 1)
    sample_rows = pl.BlockSpec((ROW_TILE, d_br), lambda t: (0, 0))
    layer_ops = (norm_g, w_s, b_s_t, ln_g, ln_b, w_conv, wpool, pool_scale, wa, wb, wc, wo)
    return pl.pallas_call(
        body,
        grid=(n_prompt + 1,),
        in_specs=[smem, smem, tile, _whole(conv_s, (layer,)), _whole(pool_s, (layer,)),
                  _whole(norm_g, (layer,)), _whole(win)]
        + [_whole(a, (layer,)) for a in layer_ops[1:]] + [c.in_spec for c in casts],
        out_specs=[
            tile,
            pl.BlockSpec((None, CONV_W - 1, d_br), lambda t: (seq_of(t), 0, 0)),
            pl.BlockSpec((None, POOL_BUF, d_br), lambda t: (seq_of(t), 0, 0)),
            sample_rows, sample_rows, sample_rows,
        ] + [c.out_spec for c in casts],
        out_shape=[
            jax.ShapeDtypeStruct((rows, d_model), F32),
            jax.ShapeDtypeStruct((batch, CONV_W - 1, d_br), F32),
            jax.ShapeDtypeStruct((batch, POOL_BUF, d_br), F32),
            jax.ShapeDtypeStruct((ROW_TILE, d_br), F32),
            jax.ShapeDtypeStruct((ROW_TILE, d_br), F32),
            jax.ShapeDtypeStruct((ROW_TILE, d_br), F32),
        ] + [c.out_shape for c in casts],
        scratch_shapes=[
            pltpu.VMEM((HALO + ts, d_br), F32),
            pltpu.VMEM((HALO + ts, d_br), F32),
            pltpu.VMEM((ts, d_br), F32),
        ],
        compiler_params=pltpu.CompilerParams(
            dimension_semantics=("arbitrary",), vmem_limit_bytes=VMEM_LIMIT_BYTES),
        name=f"mixer_l{layer}",
    )(ws_small, bs_small, x, conv_s, pool_s, norm_g, win, *layer_ops[1:], *[c.operand for c in casts])


def kernel(x_prompt, x_sample, state_conv, state_pool, norm_g, w_ffn_gu, w_ffn_down, w_in, w_s, b_s,
           ln_g, ln_b, w_conv, w_pool, pool_scale, w_a_out, w_b_out, w_c_out, w_o):
    depth = w_in.shape[0]
    batch, seq, d_model = x_prompt.shape
    nb, steps, _ = x_sample.shape
    assert (batch * seq) % ROW_TILE == 0 and nb * steps == ROW_TILE
    n_prompt = batch * seq // ROW_TILE

    wpool = w_pool.astype(BF16)
    wa, wb, wc, wo = (w.astype(BF16) for w in (w_a_out, w_b_out, w_c_out, w_o))
    lng, lnb, pscale = (a[:, None, :] for a in (ln_g, ln_b, pool_scale))
    b_s_t = jnp.swapaxes(b_s, 1, 2)
    ws_small = w_s[:, :, :steps, :steps].reshape(depth, -1)
    bs_small = b_s[:, :, :steps].reshape(depth, -1)
    conv_s = jnp.swapaxes(state_conv, 1, 2)
    pool_s = jnp.swapaxes(state_pool, 1, 2)

    x_in = (x_prompt.reshape(batch * seq, d_model),
            jnp.swapaxes(x_sample, 0, 1).reshape(steps * nb, d_model))
    wgu, wdown = w_ffn_gu[0, 0].astype(BF16), w_ffn_down[0, 0].astype(BF16)

    conv_p_out, pool_p_out, conv_s_out, pool_s_out, v_s_out = [], [], [], [], []
    for l in range(depth):
        x, win = _ffn(x_in, norm_g, wgu, wdown, [_Cast(w_in, (l,), n_prompt)], l, 0, n_prompt, False)
        x, cp, pp, vs, zs, ps, wgu, wdown = _mixer(
            x, batch, steps, conv_s, pool_s, ws_small, bs_small, norm_g, win, w_s, b_s_t, lng, lnb,
            w_conv, wpool, pscale, wa, wb, wc, wo,
            [_Cast(w_ffn_gu, (l, 1), n_prompt), _Cast(w_ffn_down, (l, 1), n_prompt)], l, n_prompt)
        if l + 1 < depth:
            x, wgu, wdown = _ffn((x,), norm_g, wgu, wdown,
                                 [_Cast(w_ffn_gu, (l + 1, 0), n_prompt), _Cast(w_ffn_down, (l + 1, 0), n_prompt)],
                                 l, 1, n_prompt, False)
            x_in = (x,)
        else:
            y_prompt_rows, y_sample_rows = _ffn((x,), norm_g, wgu, wdown, [], l, 1, n_prompt, True)
        conv_p_out.append(cp)
        pool_p_out.append(pp)
        z_ext = jnp.concatenate([conv_s[l], zs.reshape(steps, nb, -1)], axis=0)[-(CONV_W - 1):]
        p_ext = jnp.concatenate([pool_s[l], ps.reshape(steps, nb, -1)], axis=0)[-POOL_BUF:]
        conv_s_out.append(jnp.swapaxes(z_ext, 0, 1))
        pool_s_out.append(jnp.swapaxes(p_ext, 0, 1))
        v_s_out.append(jnp.swapaxes(vs.reshape(steps, nb, -1), 0, 1))

    y_prompt = y_prompt_rows.reshape(batch, seq, d_model)
    y_sample = jnp.swapaxes(y_sample_rows.reshape(steps, nb, d_model), 0, 1)
    return (y_prompt, y_sample, jnp.stack(conv_p_out), jnp.stack(pool_p_out),
            jnp.stack(conv_s_out), jnp.stack(pool_s_out), jnp.stack(v_s_out))
```

```python
import functools

import jax
import jax.numpy as jnp
from jax import lax
from jax.experimental import pallas as pl
from jax.experimental.pallas import tpu as pltpu

EPS = 1e-6
PAST_LEN = 16384
CHUNK = 128
N_GROUPS = 4
POOL_WINDOWS = (2, 4, 8, 16)
POOL_BUF = max(POOL_WINDOWS) - 1
CONV_W = 3

SUBLANES = 8
LANES = 128
BF16_ROWS = 16
SKEW_PERIOD = 8
HALO = 16
FFN_COL_CHUNK = 512
ROW_TILE = 512
MXU_COLS = 256
VMEM_LIMIT_BYTES = 60 * 1024 * 1024

F32 = jnp.float32
BF16 = jnp.bfloat16


def _rms(x, g):
    return x * lax.rsqrt(jnp.mean(x * x, axis=-1, keepdims=True) + EPS) * g


def _layer_norm(x, g, b):
    xc = x - jnp.mean(x, axis=-1, keepdims=True)
    return xc * lax.rsqrt(jnp.mean(xc * xc, axis=-1, keepdims=True) + EPS) * g + b


def _dot(a, b):
    return jnp.dot(a, b, preferred_element_type=F32)


def _resident(block_shape, index_map):
    return pl.BlockSpec(block_shape, index_map, pipeline_mode=pl.Buffered(1))


def _whole(a, lead=()):
    nd = a.ndim - len(lead)
    return _resident((None,) * len(lead) + a.shape[len(lead):], lambda t: tuple(lead) + (0,) * nd)


def _skewed_cols(cols):
    return cols + LANES if cols % (SKEW_PERIOD * LANES) == 0 else cols


def _skew(w):
    pad = _skewed_cols(w.shape[-1]) - w.shape[-1]
    return jnp.pad(w, [(0, 0)] * (w.ndim - 1) + [(0, pad)]) if pad else w


class _Cast:
    def __init__(self, w, lead, n_prompt):
        rows, cols = w.shape[-2:]
        share = 1
        while n_prompt % share or rows % (n_prompt // share) or (rows // (n_prompt // share)) % BF16_ROWS:
            share *= 2
            assert share <= n_prompt
        n_blocks = n_prompt // share
        blk = rows // n_blocks

        def block(t):
            return jnp.minimum(t // share, n_blocks - 1)

        self.operand = w
        self.in_spec = pl.BlockSpec((None,) * len(lead) + (blk, cols), lambda t: tuple(lead) + (block(t), 0))
        self.out_spec = pl.BlockSpec((blk, _skewed_cols(cols)), lambda t: (block(t), 0))
        self.out_shape = jax.ShapeDtypeStruct((rows, _skewed_cols(cols)), BF16)


def _run_casts(cast_in, cast_out):
    for src, dst in zip(cast_in, cast_out):
        rows, cols = src.shape
        dst[:, :cols] = src[...].astype(BF16)
        if dst.shape[1] > cols:
            dst[:, cols:] = jnp.zeros((rows, dst.shape[1] - cols), BF16)


def _ffn_body(*refs, d_ff, pre, post, two_in, two_out, n_cast, n_prompt):
    it = iter(refs)
    xp_ref = next(it)
    xs_ref = next(it) if two_in else None
    ng_ref, wgu_ref, wdown_ref = next(it), next(it), next(it)
    cast_in = [next(it) for _ in range(n_cast)]
    op_ref = next(it)
    os_ref = next(it) if two_out else None
    cast_out = [next(it) for _ in range(n_cast)]

    t = pl.program_id(0)
    x = xp_ref[...]
    if two_in:
        x = jnp.where(t < n_prompt, x, xs_ref[...])
    ng = ng_ref[...]
    h = _rms(x, ng[pre:pre + 1]).astype(BF16)
    acc = None
    for c0 in range(0, d_ff, FFN_COL_CHUNK):
        c1 = min(c0 + FFN_COL_CHUNK, d_ff)
        g = _dot(h, wgu_ref[:, c0:c1])
        u = _dot(h, wgu_ref[:, d_ff + c0:d_ff + c1])
        a = (g * jax.nn.sigmoid(g) * u).astype(BF16)
        part = _dot(a, wdown_ref[c0:c1, :x.shape[1]])
        acc = part if acc is None else acc + part
    y = x + 0.5 * _rms(acc, ng[post:post + 1])
    if two_out:
        @pl.when(t < n_prompt)
        def _():
            op_ref[...] = y

        @pl.when(t == n_prompt)
        def _():
            os_ref[...] = y
    else:
        op_ref[...] = y
    _run_casts(cast_in, cast_out)


def _ffn(xs_in, norm_g, wgu, wdown, casts, layer, which, n_prompt, two_out):
    two_in = len(xs_in) == 2
    d = xs_in[0].shape[1]
    d_ff = wdown.shape[0]
    tile = pl.BlockSpec((ROW_TILE, d), lambda t: (t, 0))
    prompt_tile = pl.BlockSpec((ROW_TILE, d), lambda t: (jnp.minimum(t, n_prompt - 1), 0))
    sample_tile = pl.BlockSpec((ROW_TILE, d), lambda t: (0, 0))
    rows = (n_prompt + 1) * ROW_TILE
    body = functools.partial(_ffn_body, d_ff=d_ff, pre=4 * which, post=4 * which + 1, two_in=two_in,
                             two_out=two_out, n_cast=len(casts), n_prompt=n_prompt)
    if two_out:
        x_out_specs = [prompt_tile, sample_tile]
        x_out_shapes = [jax.ShapeDtypeStruct((n_prompt * ROW_TILE, d), F32),
                        jax.ShapeDtypeStruct((ROW_TILE, d), F32)]
    else:
        x_out_specs = [tile]
        x_out_shapes = [jax.ShapeDtypeStruct((rows, d), F32)]
    return pl.pallas_call(
        body,
        grid=(n_prompt + 1,),
        in_specs=([prompt_tile, sample_tile] if two_in else [tile])
        + [_whole(norm_g, (layer,)), _whole(wgu), _whole(wdown)] + [c.in_spec for c in casts],
        out_specs=x_out_specs + [c.out_spec for c in casts],
        out_shape=x_out_shapes + [c.out_shape for c in casts],
        compiler_params=pltpu.CompilerParams(
            dimension_semantics=("arbitrary",), vmem_limit_bytes=VMEM_LIMIT_BYTES),
        name=f"ffn{which}_l{layer}",
    )(*xs_in, norm_g, wgu, wdown, *[c.operand for c in casts])


def _project(h, win_ref, lo, hi):
    return _dot(h, win_ref[:, lo:hi])


def _join(parts):
    return jnp.concatenate(parts, axis=-1)


def _mixer_prompt_tile(j, is_last_j, x_ref, ng_ref, win_ref, ws_ref, bst_ref, lng_ref, lnb_ref, wconv_ref,
                       wpool_ref, pscale_ref, wa_ref, wb_ref, wc_ref, wo_ref,
                       o_ref, conv_out_ref, pool_out_ref, zext_ref, pext_ref, ya_ref, psum_ref,
                       *, ts, d_model, d_br):
    gw = d_br // N_GROUPS

    @pl.when(j == 0)
    def _():
        zext_ref[0:HALO, :] = jnp.zeros((HALO, d_br), F32)
        pext_ref[0:HALO, :] = jnp.zeros((HALO, d_br), F32)
        psum_ref[:, 0:SUBLANES, :] = jnp.zeros((N_GROUPS, SUBLANES, gw), F32)

    x = x_ref[...]
    ng = ng_ref[...]
    h = _rms(x, ng[2:3]).astype(BF16)

    def slabs(lo, hi):
        return [functools.partial(_project, h, win_ref, c, c + MXU_COLS) for c in range(lo, hi, MXU_COLS)]

    v_raw = _project(h, win_ref, d_br, 2 * d_br)
    u_raw = _project(h, win_ref, 0, d_br)
    u = jax.nn.gelu(u_raw)
    vb = _layer_norm(jax.nn.gelu(v_raw), lng_ref[...], lnb_ref[...]).astype(BF16)

    rows_i = lax.broadcasted_iota(jnp.int32, (CHUNK, CHUNK), 0)
    cols_i = lax.broadcasted_iota(jnp.int32, (CHUNK, CHUNK), 1)
    causal = cols_i <= rows_i

    def spatial(g, n):
        r0, c0 = n * CHUNK, g * gw
        wsg = jnp.where(causal, ws_ref[g], 0.0).astype(BF16)
        mixed = _dot(wsg, vb[r0:r0 + CHUNK, c0:c0 + gw]) + bst_ref[:, g:g + 1]
        ya_ref[r0:r0 + CHUNK, c0:c0 + gw] = u[r0:r0 + CHUNK, c0:c0 + gw] * mixed

    small = [functools.partial(spatial, g, n) for g in range(N_GROUPS) for n in range(ts // CHUNK)]
    big = slabs(2 * d_br, 6 * d_br + 2 * d_model)
    done = []
    for k, slab in enumerate(big):
        done.append(slab())
        if k < len(small):
            small[k]()
    for rest in small[len(big):]:
        rest()
    per = d_br // MXU_COLS
    xb, gb, gc, p = (_join(done[i * per:(i + 1) * per]) for i in range(4))
    per_gate = d_model // MXU_COLS
    gate_a = jax.nn.sigmoid(_join(done[4 * per:4 * per + per_gate]))
    gate_b = jax.nn.sigmoid(_join(done[4 * per + per_gate:]))
    merged = gate_a * _dot(ya_ref[...].astype(BF16), wa_ref[:, :d_model])

    z = gc * xb
    zext_ref[HALO:HALO + ts, :] = z
    wconv = wconv_ref[...]
    conv = wconv[0:1] * zext_ref[HALO - 2:HALO - 2 + ts, :]
    conv = conv + wconv[1:2] * zext_ref[HALO - 1:HALO - 1 + ts, :]
    conv = conv + wconv[2:3] * z
    yb = (gb * conv).astype(BF16)

    pext_ref[HALO:HALO + ts, :] = p
    pos = j * ts + lax.broadcasted_iota(jnp.int32, (ts, gw), 0)
    gate_c_parts, yc_parts = [], []
    for g, (w, slab) in enumerate(zip(POOL_WINDOWS, slabs(6 * d_br + 2 * d_model, 6 * d_br + 3 * d_model))):
        c0 = g * gw
        run = pext_ref[:, c0:c0 + gw]
        shift = 1
        while shift < w:
            psum_ref[g, SUBLANES:, :] = run
            run = run + psum_ref[g, SUBLANES - shift:SUBLANES - shift + HALO + ts, :]
            shift *= 2
        cnt = jnp.minimum(pos + 1, w).astype(F32)
        d = run[HALO:] / cnt - p[:, c0:c0 + gw]
        gate_c_parts.append(slab())
        yc_parts.append(_dot(d.astype(BF16), wpool_ref[g]))
    yc = _join(yc_parts) * pscale_ref[...]

    merged = merged + gate_b * _dot(yb, wb_ref[:, :d_model])
    merged = merged + jax.nn.sigmoid(_join(gate_c_parts)) * _dot(yc.astype(BF16), wc_ref[:, :d_model])

    out = _dot(merged.astype(BF16), wo_ref[:, :d_model])
    o_ref[...] = x + _rms(out, ng[3:4])

    @pl.when(is_last_j)
    def _():
        conv_out_ref[...] = zext_ref[HALO + ts - (CONV_W - 1):HALO + ts, :]
        pool_out_ref[...] = pext_ref[HALO + ts - POOL_BUF:HALO + ts, :]

    zext_ref[0:HALO, :] = zext_ref[ts:ts + HALO, :]
    pext_ref[0:HALO, :] = pext_ref[ts:ts + HALO, :]


def _mixer_sample_tile(ws_ref, bs_ref, x_ref, conv_ref, pool_ref, ng_ref, win_ref, lng_ref, lnb_ref,
                       wconv_ref, wpool_ref, pscale_ref, wa_ref, wb_ref, wc_ref, wo_ref,
                       o_ref, v_out_ref, z_out_ref, p_out_ref,
                       *, layer, nb, steps, d_model, d_br, start_pos):
    gw = d_br // N_GROUPS
    x = x_ref[...]
    ng = ng_ref[...]
    h = _rms(x, ng[2:3]).astype(BF16)

    def blk(a, i):
        return a[i * nb:(i + 1) * nb]

    u = jax.nn.gelu(_project(h, win_ref, 0, d_br))
    v = _layer_norm(jax.nn.gelu(_project(h, win_ref, d_br, 2 * d_br)), lng_ref[...], lnb_ref[...])
    v_out_ref[...] = v
    ya_rows = []
    for t in range(steps):
        cols = []
        for g in range(N_GROUPS):
            c0 = g * gw
            mixed = None
            for s in range(t + 1):
                term = ws_ref[layer, g * steps * steps + t * steps + s] * blk(v, s)[:, c0:c0 + gw]
                mixed = term if mixed is None else mixed + term
            cols.append(mixed + bs_ref[layer, g * steps + t])
        ya_rows.append(blk(u, t) * _join(cols))
    ya = jnp.concatenate(ya_rows, axis=0)

    z = _project(h, win_ref, 4 * d_br, 5 * d_br) * _project(h, win_ref, 2 * d_br, 3 * d_br)
    z_out_ref[...] = z
    z_ext = [conv_ref[k] for k in range(CONV_W - 1)] + [blk(z, i) for i in range(steps)]
    wconv = wconv_ref[...]
    conv_rows = []
    for i in range(steps):
        c = wconv[0:1] * z_ext[i]
        for k in range(1, CONV_W):
            c = c + wconv[k:k + 1] * z_ext[i + k]
        conv_rows.append(c)
    yb = _project(h, win_ref, 3 * d_br, 4 * d_br) * jnp.concatenate(conv_rows, axis=0)

    p = _project(h, win_ref, 5 * d_br, 6 * d_br)
    p_out_ref[...] = p
    p_ext = [pool_ref[k] for k in range(POOL_BUF)] + [blk(p, i) for i in range(steps)]
    d_groups = []
    for g, w in enumerate(POOL_WINDOWS):
        c0 = g * gw
        rows_g = []
        for i in range(steps):
            s = p_ext[POOL_BUF + i][:, c0:c0 + gw]
            for k in range(1, w):
                s = s + p_ext[POOL_BUF + i - k][:, c0:c0 + gw]
            cnt = float(min(start_pos + i + 1, w))
            rows_g.append(s / cnt - p_ext[POOL_BUF + i][:, c0:c0 + gw])
        d_groups.append(jnp.concatenate(rows_g, axis=0))
    yc = _join([_dot(d_groups[g].astype(BF16), wpool_ref[g]) for g in range(N_GROUPS)]) * pscale_ref[...]

    merged = None
    for k, (y, w_ref) in enumerate(((ya, wa_ref), (yb, wb_ref), (yc, wc_ref))):
        lo = 6 * d_br + k * d_model
        term = jax.nn.sigmoid(_project(h, win_ref, lo, lo + d_model)) * _dot(y.astype(BF16), w_ref[:, :d_model])
        merged = term if merged is None else merged + term
    out = _dot(merged.astype(BF16), wo_ref[:, :d_model])
    o_ref[...] = x + _rms(out, ng[3:4])


def _mixer_body(*refs, layer, n_prompt, nj, n_cast, ts, nb, steps, d_model, d_br):
    it = iter(refs)
    ws_sm, bs_sm, x_ref, conv_s_ref, pool_s_ref = (next(it) for _ in range(5))
    (ng_ref, win_ref, ws_ref, bst_ref, lng_ref, lnb_ref, wconv_ref, wpool_ref, pscale_ref,
     wa_ref, wb_ref, wc_ref, wo_ref) = (next(it) for _ in range(13))
    cast_in = [next(it) for _ in range(n_cast)]
    o_ref, conv_out_ref, pool_out_ref, v_out_ref, z_out_ref, p_out_ref = (next(it) for _ in range(6))
    cast_out = [next(it) for _ in range(n_cast)]
    zext_ref, pext_ref, ya_ref, psum_ref = next(it), next(it), next(it), next(it)

    t = pl.program_id(0)

    @pl.when(t < n_prompt)
    def _():
        j = lax.rem(t, nj)
        _mixer_prompt_tile(j, j == nj - 1, x_ref, ng_ref, win_ref, ws_ref, bst_ref, lng_ref, lnb_ref,
                           wconv_ref, wpool_ref, pscale_ref, wa_ref, wb_ref, wc_ref, wo_ref,
                           o_ref, conv_out_ref, pool_out_ref, zext_ref, pext_ref, ya_ref, psum_ref,
                           ts=ts, d_model=d_model, d_br=d_br)

    @pl.when(t == n_prompt)
    def _():
        _mixer_sample_tile(ws_sm, bs_sm, x_ref, conv_s_ref, pool_s_ref, ng_ref, win_ref, lng_ref, lnb_ref,
                           wconv_ref, wpool_ref, pscale_ref, wa_ref, wb_ref, wc_ref, wo_ref,
                           o_ref, v_out_ref, z_out_ref, p_out_ref,
                           layer=layer, nb=nb, steps=steps, d_model=d_model, d_br=d_br, start_pos=PAST_LEN)

    _run_casts(cast_in, cast_out)


def _mixer(x, batch, steps, conv_s, pool_s, ws_small, bs_small, norm_g, win, w_s, b_s_t, ln_g, ln_b,
           w_conv, wpool, pool_scale, wa, wb, wc, wo, casts, layer, n_prompt):
    rows, d_model = x.shape
    d_br = wa.shape[1]
    ts = ROW_TILE
    nj = n_prompt // batch
    nb = ROW_TILE // steps
    assert n_prompt % batch == 0 and ts % CHUNK == 0 and ts >= HALO
    assert steps <= CHUNK and nb * steps == ROW_TILE and nb % SUBLANES == 0
    assert d_br % MXU_COLS == 0 and d_model // MXU_COLS == N_GROUPS
    body = functools.partial(_mixer_body, layer=layer, n_prompt=n_prompt, nj=nj, n_cast=len(casts),
                             ts=ts, nb=nb, steps=steps, d_model=d_model, d_br=d_br)
    smem = pl.BlockSpec(memory_space=pltpu.SMEM)
    tile = pl.BlockSpec((ts, d_model), lambda t: (t, 0))
    seq_of = lambda t: jnp.minimum(t // nj, batch - 1)
    sample_rows = pl.BlockSpec((ROW_TILE, d_br), lambda t: (0, 0))
    layer_ops = (norm_g, w_s, b_s_t, ln_g, ln_b, w_conv, wpool, pool_scale, wa, wb, wc, wo)
    return pl.pallas_call(
        body,
        grid=(n_prompt + 1,),
        in_specs=[smem, smem, tile, _whole(conv_s, (layer,)), _whole(pool_s, (layer,)),
                  _whole(norm_g, (layer,)), _whole(win)]
        + [_whole(a, (layer,)) for a in layer_ops[1:]] + [c.in_spec for c in casts],
        out_specs=[
            tile,
            pl.BlockSpec((None, CONV_W - 1, d_br), lambda t: (seq_of(t), 0, 0)),
            pl.BlockSpec((None, POOL_BUF, d_br), lambda t: (seq_of(t), 0, 0)),
            sample_rows, sample_rows, sample_rows,
        ] + [c.out_spec for c in casts],
        out_shape=[
            jax.ShapeDtypeStruct((rows, d_model), F32),
            jax.ShapeDtypeStruct((batch, CONV_W - 1, d_br), F32),
            jax.ShapeDtypeStruct((batch, POOL_BUF, d_br), F32),
            jax.ShapeDtypeStruct((ROW_TILE, d_br), F32),
            jax.ShapeDtypeStruct((ROW_TILE, d_br), F32),
            jax.ShapeDtypeStruct((ROW_TILE, d_br), F32),
        ] + [c.out_shape for c in casts],
        scratch_shapes=[
            pltpu.VMEM((HALO + ts, d_br), F32),
            pltpu.VMEM((HALO + ts, d_br), F32),
            pltpu.VMEM((ts, d_br), F32),
            pltpu.VMEM((N_GROUPS, SUBLANES + HALO + ts, d_br // N_GROUPS), F32),
        ],
        compiler_params=pltpu.CompilerParams(
            dimension_semantics=("arbitrary",), vmem_limit_bytes=VMEM_LIMIT_BYTES),
        name=f"mixer_l{layer}",
    )(ws_small, bs_small, x, conv_s, pool_s, norm_g, win, *layer_ops[1:], *[c.operand for c in casts])


def kernel(x_prompt, x_sample, state_conv, state_pool, norm_g, w_ffn_gu, w_ffn_down, w_in, w_s, b_s,
           ln_g, ln_b, w_conv, w_pool, pool_scale, w_a_out, w_b_out, w_c_out, w_o):
    depth = w_in.shape[0]
    batch, seq, d_model = x_prompt.shape
    nb, steps, _ = x_sample.shape
    assert (batch * seq) % ROW_TILE == 0 and nb * steps == ROW_TILE
    n_prompt = batch * seq // ROW_TILE

    wpool = w_pool.astype(BF16)
    wa, wb, wc, wo = (_skew(w.astype(BF16)) for w in (w_a_out, w_b_out, w_c_out, w_o))
    lng, lnb, pscale = (a[:, None, :] for a in (ln_g, ln_b, pool_scale))
    b_s_t = jnp.swapaxes(b_s, 1, 2)
    ws_small = w_s[:, :, :steps, :steps].reshape(depth, -1)
    bs_small = b_s[:, :, :steps].reshape(depth, -1)
    conv_s = jnp.swapaxes(state_conv, 1, 2)
    pool_s = jnp.swapaxes(state_pool, 1, 2)

    x_in = (x_prompt.reshape(batch * seq, d_model),
            jnp.swapaxes(x_sample, 0, 1).reshape(steps * nb, d_model))
    wgu, wdown = _skew(w_ffn_gu[0, 0].astype(BF16)), _skew(w_ffn_down[0, 0].astype(BF16))

    conv_p_out, pool_p_out, conv_s_out, pool_s_out, v_s_out = [], [], [], [], []
    for l in range(depth):
        x, win = _ffn(x_in, norm_g, wgu, wdown, [_Cast(w_in, (l,), n_prompt)], l, 0, n_prompt, False)
        x, cp, pp, vs, zs, ps, wgu, wdown = _mixer(
            x, batch, steps, conv_s, pool_s, ws_small, bs_small, norm_g, win, w_s, b_s_t, lng, lnb,
            w_conv, wpool, pscale, wa, wb, wc, wo,
            [_Cast(w_ffn_gu, (l, 1), n_prompt), _Cast(w_ffn_down, (l, 1), n_prompt)], l, n_prompt)
        if l + 1 < depth:
            x, wgu, wdown = _ffn((x,), norm_g, wgu, wdown,
                                 [_Cast(w_ffn_gu, (l + 1, 0), n_prompt), _Cast(w_ffn_down, (l + 1, 0), n_prompt)],
                                 l, 1, n_prompt, False)
            x_in = (x,)
        else:
            y_prompt_rows, y_sample_rows = _ffn((x,), norm_g, wgu, wdown, [], l, 1, n_prompt, True)
        conv_p_out.append(cp)
        pool_p_out.append(pp)
        z_ext = jnp.concatenate([conv_s[l], zs.reshape(steps, nb, -1)], axis=0)[-(CONV_W - 1):]
        p_ext = jnp.concatenate([pool_s[l], ps.reshape(steps, nb, -1)], axis=0)[-POOL_BUF:]
        conv_s_out.append(jnp.swapaxes(z_ext, 0, 1))
        pool_s_out.append(jnp.swapaxes(p_ext, 0, 1))
        v_s_out.append(jnp.swapaxes(vs.reshape(steps, nb, -1), 0, 1))

    y_prompt = y_prompt_rows.reshape(batch, seq, d_model)
    y_sample = jnp.swapaxes(y_sample_rows.reshape(steps, nb, d_model), 0, 1)
    return (y_prompt, y_sample, jnp.stack(conv_p_out), jnp.stack(pool_p_out),
            jnp.stack(conv_s_out), jnp.stack(pool_s_out), jnp.stack(v_s_out))
```

```python
import functools

import jax
import jax.numpy as jnp
from jax import lax
from jax.experimental import pallas as pl
from jax.experimental.pallas import tpu as pltpu

EPS = 1e-6
PAST_LEN = 16384
CHUNK = 128
N_GROUPS = 4
POOL_WINDOWS = (2, 4, 8, 16)
POOL_BUF = max(POOL_WINDOWS) - 1
CONV_W = 3

SUBLANES = 8
LANES = 128
BF16_ROWS = 16
SKEW_PERIOD = 8
HALO = 16
FFN_COL_CHUNK = 512
ROW_TILE = 512
MXU_COLS = 256
VMEM_LIMIT_BYTES = 60 * 1024 * 1024

F32 = jnp.float32
BF16 = jnp.bfloat16


def _rms(x, g):
    return x * lax.rsqrt(jnp.mean(x * x, axis=-1, keepdims=True) + EPS) * g


def _layer_norm(x, g, b):
    xc = x - jnp.mean(x, axis=-1, keepdims=True)
    return xc * lax.rsqrt(jnp.mean(xc * xc, axis=-1, keepdims=True) + EPS) * g + b


def _dot(a, b):
    return jnp.dot(a, b, preferred_element_type=F32)


def _resident(block_shape, index_map):
    return pl.BlockSpec(block_shape, index_map, pipeline_mode=pl.Buffered(1))


def _whole(a, lead=()):
    nd = a.ndim - len(lead)
    return _resident((None,) * len(lead) + a.shape[len(lead):], lambda t: tuple(lead) + (0,) * nd)


def _skewed_cols(cols):
    return cols + LANES if cols % (SKEW_PERIOD * LANES) == 0 else cols


def _skew(w):
    pad = _skewed_cols(w.shape[-1]) - w.shape[-1]
    return jnp.pad(w, [(0, 0)] * (w.ndim - 1) + [(0, pad)]) if pad else w


class _Cast:
    def __init__(self, w, lead, n_prompt):
        rows, cols = w.shape[-2:]
        share = 1
        while n_prompt % share or rows % (n_prompt // share) or (rows // (n_prompt // share)) % BF16_ROWS:
            share *= 2
            assert share <= n_prompt
        n_blocks = n_prompt // share
        blk = rows // n_blocks

        def block(t):
            return jnp.minimum(t // share, n_blocks - 1)

        self.operand = w
        self.in_spec = pl.BlockSpec((None,) * len(lead) + (blk, cols), lambda t: tuple(lead) + (block(t), 0))
        self.out_spec = pl.BlockSpec((blk, _skewed_cols(cols)), lambda t: (block(t), 0))
        self.out_shape = jax.ShapeDtypeStruct((rows, _skewed_cols(cols)), BF16)


def _run_casts(cast_in, cast_out):
    for src, dst in zip(cast_in, cast_out):
        rows, cols = src.shape
        dst[:, :cols] = src[...].astype(BF16)
        if dst.shape[1] > cols:
            dst[:, cols:] = jnp.zeros((rows, dst.shape[1] - cols), BF16)


def _ffn_body(*refs, d_ff, pre, post, two_in, two_out, n_cast, n_prompt):
    it = iter(refs)
    xp_ref = next(it)
    xs_ref = next(it) if two_in else None
    ng_ref, wgu_ref, wdown_ref = next(it), next(it), next(it)
    cast_in = [next(it) for _ in range(n_cast)]
    op_ref = next(it)
    os_ref = next(it) if two_out else None
    cast_out = [next(it) for _ in range(n_cast)]

    t = pl.program_id(0)
    x = xp_ref[...]
    if two_in:
        x = jnp.where(t < n_prompt, x, xs_ref[...])
    ng = ng_ref[...]
    h = _rms(x, ng[pre:pre + 1]).astype(BF16)
    acc = None
    for c0 in range(0, d_ff, FFN_COL_CHUNK):
        c1 = min(c0 + FFN_COL_CHUNK, d_ff)
        g = _dot(h, wgu_ref[:, c0:c1])
        u = _dot(h, wgu_ref[:, d_ff + c0:d_ff + c1])
        a = (g * jax.nn.sigmoid(g) * u).astype(BF16)
        part = _dot(a, wdown_ref[c0:c1, :x.shape[1]])
        acc = part if acc is None else acc + part
    y = x + 0.5 * _rms(acc, ng[post:post + 1])
    if two_out:
        @pl.when(t < n_prompt)
        def _():
            op_ref[...] = y

        @pl.when(t == n_prompt)
        def _():
            os_ref[...] = y
    else:
        op_ref[...] = y
    _run_casts(cast_in, cast_out)


def _ffn(xs_in, norm_g, wgu, wdown, casts, layer, which, n_prompt, two_out):
    two_in = len(xs_in) == 2
    d = xs_in[0].shape[1]
    d_ff = wdown.shape[0]
    tile = pl.BlockSpec((ROW_TILE, d), lambda t: (t, 0))
    prompt_tile = pl.BlockSpec((ROW_TILE, d), lambda t: (jnp.minimum(t, n_prompt - 1), 0))
    sample_tile = pl.BlockSpec((ROW_TILE, d), lambda t: (0, 0))
    rows = (n_prompt + 1) * ROW_TILE
    body = functools.partial(_ffn_body, d_ff=d_ff, pre=4 * which, post=4 * which + 1, two_in=two_in,
                             two_out=two_out, n_cast=len(casts), n_prompt=n_prompt)
    if two_out:
        x_out_specs = [prompt_tile, sample_tile]
        x_out_shapes = [jax.ShapeDtypeStruct((n_prompt * ROW_TILE, d), F32),
                        jax.ShapeDtypeStruct((ROW_TILE, d), F32)]
    else:
        x_out_specs = [tile]
        x_out_shapes = [jax.ShapeDtypeStruct((rows, d), F32)]
    return pl.pallas_call(
        body,
        grid=(n_prompt + 1,),
        in_specs=([prompt_tile, sample_tile] if two_in else [tile])
        + [_whole(norm_g, (layer,)), _whole(wgu), _whole(wdown)] + [c.in_spec for c in casts],
        out_specs=x_out_specs + [c.out_spec for c in casts],
        out_shape=x_out_shapes + [c.out_shape for c in casts],
        compiler_params=pltpu.CompilerParams(
            dimension_semantics=("arbitrary",), vmem_limit_bytes=VMEM_LIMIT_BYTES),
        name=f"ffn{which}_l{layer}",
    )(*xs_in, norm_g, wgu, wdown, *[c.operand for c in casts])


def _project(h, win_ref, lo, hi):
    return _dot(h, win_ref[:, lo:hi])


def _join(parts):
    return jnp.concatenate(parts, axis=-1)


def _mixer_prompt_tile(j, is_last_j, x_ref, ng_ref, win_ref, ws_ref, bst_ref, lng_ref, lnb_ref, wconv_ref,
                       wpool_ref, pscale_ref, wa_ref, wb_ref, wc_ref, wo_ref,
                       o_ref, conv_out_ref, pool_out_ref, zext_ref, pext_ref, ya_ref, psum_ref,
                       *, ts, d_model, d_br):
    gw = d_br // N_GROUPS

    @pl.when(j == 0)
    def _():
        zext_ref[0:HALO, :] = jnp.zeros((HALO, d_br), F32)
        pext_ref[0:HALO, :] = jnp.zeros((HALO, d_br), F32)
        psum_ref[:, 0:SUBLANES, :] = jnp.zeros((N_GROUPS, SUBLANES, gw), F32)

    x = x_ref[...]
    ng = ng_ref[...]
    h = _rms(x, ng[2:3]).astype(BF16)

    def slabs(lo, hi):
        return [functools.partial(_project, h, win_ref, c, c + MXU_COLS) for c in range(lo, hi, MXU_COLS)]

    v_raw = _project(h, win_ref, d_br, 2 * d_br)
    u_raw = _project(h, win_ref, 0, d_br)
    u = jax.nn.gelu(u_raw)
    vb = _layer_norm(jax.nn.gelu(v_raw), lng_ref[...], lnb_ref[...]).astype(BF16)

    rows_i = lax.broadcasted_iota(jnp.int32, (CHUNK, CHUNK), 0)
    cols_i = lax.broadcasted_iota(jnp.int32, (CHUNK, CHUNK), 1)
    causal = cols_i <= rows_i
    n_chunks = ts // CHUNK

    def spatial(g):
        c0 = g * gw
        wsg = jnp.where(causal, ws_ref[g], 0.0).astype(BF16)
        mixed = _dot(wsg, _join([vb[n * CHUNK:(n + 1) * CHUNK, c0:c0 + gw] for n in range(n_chunks)]))
        bias = bst_ref[:, g:g + 1]
        for n in range(n_chunks):
            r0 = n * CHUNK
            ya_ref[r0:r0 + CHUNK, c0:c0 + gw] = (u[r0:r0 + CHUNK, c0:c0 + gw]
                                                * (mixed[:, n * gw:(n + 1) * gw] + bias))

    big = slabs(2 * d_br, 6 * d_br + 2 * d_model)
    done = []
    for k, slab in enumerate(big):
        done.append(slab())
        if k % 2 == 1 and k // 2 < N_GROUPS:
            spatial(k // 2)
    per = d_br // MXU_COLS
    xb, gb, gc, p = (_join(done[i * per:(i + 1) * per]) for i in range(4))
    per_gate = d_model // MXU_COLS
    gate_a = jax.nn.sigmoid(_join(done[4 * per:4 * per + per_gate]))
    gate_b = jax.nn.sigmoid(_join(done[4 * per + per_gate:]))
    merged = gate_a * _dot(ya_ref[...].astype(BF16), wa_ref[:, :d_model])

    z = gc * xb
    zext_ref[HALO:HALO + ts, :] = z
    wconv = wconv_ref[...]
    conv = wconv[0:1] * zext_ref[HALO - 2:HALO - 2 + ts, :]
    conv = conv + wconv[1:2] * zext_ref[HALO - 1:HALO - 1 + ts, :]
    conv = conv + wconv[2:3] * z
    yb = (gb * conv).astype(BF16)

    pext_ref[HALO:HALO + ts, :] = p
    pos = j * ts + lax.broadcasted_iota(jnp.int32, (ts, gw), 0)
    d_groups = []
    for g, w in enumerate(POOL_WINDOWS):
        c0 = g * gw
        run = pext_ref[:, c0:c0 + gw]
        shift = 1
        while shift < w:
            psum_ref[g, SUBLANES:, :] = run
            run = run + psum_ref[g, SUBLANES - shift:SUBLANES - shift + HALO + ts, :]
            shift *= 2
        cnt = jnp.minimum(pos + 1, w).astype(F32)
        d_groups.append(run[HALO:] / cnt - p[:, c0:c0 + gw])
    gate_c_parts, yc_parts = [], []
    zero_block = jnp.zeros((gw, gw), BF16)
    for k, slab in enumerate(slabs(6 * d_br + 2 * d_model, 6 * d_br + 3 * d_model)):
        gate_c_parts.append(slab())
        if k % 2 == 0:
            w_pair = jnp.concatenate([_join([wpool_ref[k], zero_block]),
                                      _join([zero_block, wpool_ref[k + 1]])], axis=0)
            yc_parts.append(_dot(_join(d_groups[k:k + 2]).astype(BF16), w_pair))
    yc = _join(yc_parts) * pscale_ref[...]

    merged = merged + gate_b * _dot(yb, wb_ref[:, :d_model])
    merged = merged + jax.nn.sigmoid(_join(gate_c_parts)) * _dot(yc.astype(BF16), wc_ref[:, :d_model])

    out = _dot(merged.astype(BF16), wo_ref[:, :d_model])
    o_ref[...] = x + _rms(out, ng[3:4])

    @pl.when(is_last_j)
    def _():
        conv_out_ref[...] = zext_ref[HALO + ts - (CONV_W - 1):HALO + ts, :]
        pool_out_ref[...] = pext_ref[HALO + ts - POOL_BUF:HALO + ts, :]

    zext_ref[0:HALO, :] = zext_ref[ts:ts + HALO, :]
    pext_ref[0:HALO, :] = pext_ref[ts:ts + HALO, :]


def _mixer_sample_tile(ws_ref, bs_ref, x_ref, conv_ref, pool_ref, ng_ref, win_ref, lng_ref, lnb_ref,
                       wconv_ref, wpool_ref, pscale_ref, wa_ref, wb_ref, wc_ref, wo_ref,
                       o_ref, v_out_ref, z_out_ref, p_out_ref,
                       *, layer, nb, steps, d_model, d_br, start_pos):
    gw = d_br // N_GROUPS
    x = x_ref[...]
    ng = ng_ref[...]
    h = _rms(x, ng[2:3]).astype(BF16)

    def blk(a, i):
        return a[i * nb:(i + 1) * nb]

    u = jax.nn.gelu(_project(h, win_ref, 0, d_br))
    v = _layer_norm(jax.nn.gelu(_project(h, win_ref, d_br, 2 * d_br)), lng_ref[...], lnb_ref[...])
    v_out_ref[...] = v
    ya_rows = []
    for t in range(steps):
        cols = []
        for g in range(N_GROUPS):
            c0 = g * gw
            mixed = None
            for s in range(t + 1):
                term = ws_ref[layer, g * steps * steps + t * steps + s] * blk(v, s)[:, c0:c0 + gw]
                mixed = term if mixed is None else mixed + term
            cols.append(mixed + bs_ref[layer, g * steps + t])
        ya_rows.append(blk(u, t) * _join(cols))
    ya = jnp.concatenate(ya_rows, axis=0)

    z = _project(h, win_ref, 4 * d_br, 5 * d_br) * _project(h, win_ref, 2 * d_br, 3 * d_br)
    z_out_ref[...] = z
    z_ext = [conv_ref[k] for k in range(CONV_W - 1)] + [blk(z, i) for i in range(steps)]
    wconv = wconv_ref[...]
    conv_rows = []
    for i in range(steps):
        c = wconv[0:1] * z_ext[i]
        for k in range(1, CONV_W):
            c = c + wconv[k:k + 1] * z_ext[i + k]
        conv_rows.append(c)
    yb = _project(h, win_ref, 3 * d_br, 4 * d_br) * jnp.concatenate(conv_rows, axis=0)

    p = _project(h, win_ref, 5 * d_br, 6 * d_br)
    p_out_ref[...] = p
    p_ext = [pool_ref[k] for k in range(POOL_BUF)] + [blk(p, i) for i in range(steps)]
    d_groups = []
    for g, w in enumerate(POOL_WINDOWS):
        c0 = g * gw
        rows_g = []
        for i in range(steps):
            s = p_ext[POOL_BUF + i][:, c0:c0 + gw]
            for k in range(1, w):
                s = s + p_ext[POOL_BUF + i - k][:, c0:c0 + gw]
            cnt = float(min(start_pos + i + 1, w))
            rows_g.append(s / cnt - p_ext[POOL_BUF + i][:, c0:c0 + gw])
        d_groups.append(jnp.concatenate(rows_g, axis=0))
    yc = _join([_dot(d_groups[g].astype(BF16), wpool_ref[g]) for g in range(N_GROUPS)]) * pscale_ref[...]

    merged = None
    for k, (y, w_ref) in enumerate(((ya, wa_ref), (yb, wb_ref), (yc, wc_ref))):
        lo = 6 * d_br + k * d_model
        term = jax.nn.sigmoid(_project(h, win_ref, lo, lo + d_model)) * _dot(y.astype(BF16), w_ref[:, :d_model])
        merged = term if merged is None else merged + term
    out = _dot(merged.astype(BF16), wo_ref[:, :d_model])
    o_ref[...] = x + _rms(out, ng[3:4])


def _mixer_body(*refs, layer, n_prompt, nj, n_cast, ts, nb, steps, d_model, d_br):
    it = iter(refs)
    ws_sm, bs_sm, x_ref, conv_s_ref, pool_s_ref = (next(it) for _ in range(5))
    (ng_ref, win_ref, ws_ref, bst_ref, lng_ref, lnb_ref, wconv_ref, wpool_ref, pscale_ref,
     wa_ref, wb_ref, wc_ref, wo_ref) = (next(it) for _ in range(13))
    cast_in = [next(it) for _ in range(n_cast)]
    o_ref, conv_out_ref, pool_out_ref, v_out_ref, z_out_ref, p_out_ref = (next(it) for _ in range(6))
    cast_out = [next(it) for _ in range(n_cast)]
    zext_ref, pext_ref, ya_ref, psum_ref = next(it), next(it), next(it), next(it)

    t = pl.program_id(0)

    @pl.when(t < n_prompt)
    def _():
        j = lax.rem(t, nj)
        _mixer_prompt_tile(j, j == nj - 1, x_ref, ng_ref, win_ref, ws_ref, bst_ref, lng_ref, lnb_ref,
                           wconv_ref, wpool_ref, pscale_ref, wa_ref, wb_ref, wc_ref, wo_ref,
                           o_ref, conv_out_ref, pool_out_ref, zext_ref, pext_ref, ya_ref, psum_ref,
                           ts=ts, d_model=d_model, d_br=d_br)

    @pl.when(t == n_prompt)
    def _():
        _mixer_sample_tile(ws_sm, bs_sm, x_ref, conv_s_ref, pool_s_ref, ng_ref, win_ref, lng_ref, lnb_ref,
                           wconv_ref, wpool_ref, pscale_ref, wa_ref, wb_ref, wc_ref, wo_ref,
                           o_ref, v_out_ref, z_out_ref, p_out_ref,
                           layer=layer, nb=nb, steps=steps, d_model=d_model, d_br=d_br, start_pos=PAST_LEN)

    _run_casts(cast_in, cast_out)


def _mixer(x, batch, steps, conv_s, pool_s, ws_small, bs_small, norm_g, win, w_s, b_s_t, ln_g, ln_b,
           w_conv, wpool, pool_scale, wa, wb, wc, wo, casts, layer, n_prompt):
    rows, d_model = x.shape
    d_br = wa.shape[1]
    ts = ROW_TILE
    nj = n_prompt // batch
    nb = ROW_TILE // steps
    assert n_prompt % batch == 0 and ts % CHUNK == 0 and ts >= HALO
    assert steps <= CHUNK and nb * steps == ROW_TILE and nb % SUBLANES == 0
    assert d_br % MXU_COLS == 0 and d_model // MXU_COLS == N_GROUPS
    body = functools.partial(_mixer_body, layer=layer, n_prompt=n_prompt, nj=nj, n_cast=len(casts),
                             ts=ts, nb=nb, steps=steps, d_model=d_model, d_br=d_br)
    smem = pl.BlockSpec(memory_space=pltpu.SMEM)
    tile = pl.BlockSpec((ts, d_model), lambda t: (t, 0))
    seq_of = lambda t: jnp.minimum(t // nj, batch - 1)
    sample_rows = pl.BlockSpec((ROW_TILE, d_br), lambda t: (0, 0))
    layer_ops = (norm_g, w_s, b_s_t, ln_g, ln_b, w_conv, wpool, pool_scale, wa, wb, wc, wo)
    return pl.pallas_call(
        body,
        grid=(n_prompt + 1,),
        in_specs=[smem, smem, tile, _whole(conv_s, (layer,)), _whole(pool_s, (layer,)),
                  _whole(norm_g, (layer,)), _whole(win)]
        + [_whole(a, (layer,)) for a in layer_ops[1:]] + [c.in_spec for c in casts],
        out_specs=[
            tile,
            pl.BlockSpec((None, CONV_W - 1, d_br), lambda t: (seq_of(t), 0, 0)),
            pl.BlockSpec((None, POOL_BUF, d_br), lambda t: (seq_of(t), 0, 0)),
            sample_rows, sample_rows, sample_rows,
        ] + [c.out_spec for c in casts],
        out_shape=[
            jax.ShapeDtypeStruct((rows, d_model), F32),
            jax.ShapeDtypeStruct((batch, CONV_W - 1, d_br), F32),
            jax.ShapeDtypeStruct((batch, POOL_BUF, d_br), F32),
            jax.ShapeDtypeStruct((ROW_TILE, d_br), F32),
            jax.ShapeDtypeStruct((ROW_TILE, d_br), F32),
            jax.ShapeDtypeStruct((ROW_TILE, d_br), F32),
        ] + [c.out_shape for c in casts],
        scratch_shapes=[
            pltpu.VMEM((HALO + ts, d_br), F32),
            pltpu.VMEM((HALO + ts, d_br), F32),
            pltpu.VMEM((ts, d_br), F32),
            pltpu.VMEM((N_GROUPS, SUBLANES + HALO + ts, d_br // N_GROUPS), F32),
        ],
        compiler_params=pltpu.CompilerParams(
            dimension_semantics=("arbitrary",), vmem_limit_bytes=VMEM_LIMIT_BYTES),
        name=f"mixer_l{layer}",
    )(ws_small, bs_small, x, conv_s, pool_s, norm_g, win, *layer_ops[1:], *[c.operand for c in casts])


def kernel(x_prompt, x_sample, state_conv, state_pool, norm_g, w_ffn_gu, w_ffn_down, w_in, w_s, b_s,
           ln_g, ln_b, w_conv, w_pool, pool_scale, w_a_out, w_b_out, w_c_out, w_o):
    depth = w_in.shape[0]
    batch, seq, d_model = x_prompt.shape
    nb, steps, _ = x_sample.shape
    assert (batch * seq) % ROW_TILE == 0 and nb * steps == ROW_TILE
    n_prompt = batch * seq // ROW_TILE

    wpool = w_pool.astype(BF16)
    wa, wb, wc, wo = (_skew(w.astype(BF16)) for w in (w_a_out, w_b_out, w_c_out, w_o))
    lng, lnb, pscale = (a[:, None, :] for a in (ln_g, ln_b, pool_scale))
    b_s_t = jnp.swapaxes(b_s, 1, 2)
    ws_small = w_s[:, :, :steps, :steps].reshape(depth, -1)
    bs_small = b_s[:, :, :steps].reshape(depth, -1)
    conv_s = jnp.swapaxes(state_conv, 1, 2)
    pool_s = jnp.swapaxes(state_pool, 1, 2)

    x_in = (x_prompt.reshape(batch * seq, d_model),
            jnp.swapaxes(x_sample, 0, 1).reshape(steps * nb, d_model))
    wgu, wdown = _skew(w_ffn_gu[0, 0].astype(BF16)), _skew(w_ffn_down[0, 0].astype(BF16))

    conv_p_out, pool_p_out, conv_s_out, pool_s_out, v_s_out = [], [], [], [], []
    for l in range(depth):
        x, win = _ffn(x_in, norm_g, wgu, wdown, [_Cast(w_in, (l,), n_prompt)], l, 0, n_prompt, False)
        x, cp, pp, vs, zs, ps, wgu, wdown = _mixer(
            x, batch, steps, conv_s, pool_s, ws_small, bs_small, norm_g, win, w_s, b_s_t, lng, lnb,
            w_conv, wpool, pscale, wa, wb, wc, wo,
            [_Cast(w_ffn_gu, (l, 1), n_prompt), _Cast(w_ffn_down, (l, 1), n_prompt)], l, n_prompt)
        if l + 1 < depth:
            x, wgu, wdown = _ffn((x,), norm_g, wgu, wdown,
                                 [_Cast(w_ffn_gu, (l + 1, 0), n_prompt), _Cast(w_ffn_down, (l + 1, 0), n_prompt)],
                                 l, 1, n_prompt, False)
            x_in = (x,)
        else:
            y_prompt_rows, y_sample_rows = _ffn((x,), norm_g, wgu, wdown, [], l, 1, n_prompt, True)
        conv_p_out.append(cp)
        pool_p_out.append(pp)
        z_ext = jnp.concatenate([conv_s[l], zs.reshape(steps, nb, -1)], axis=0)[-(CONV_W - 1):]
        p_ext = jnp.concatenate([pool_s[l], ps.reshape(steps, nb, -1)], axis=0)[-POOL_BUF:]
        conv_s_out.append(jnp.swapaxes(z_ext, 0, 1))
        pool_s_out.append(jnp.swapaxes(p_ext, 0, 1))
        v_s_out.append(jnp.swapaxes(vs.reshape(steps, nb, -1), 0, 1))

    y_prompt = y_prompt_rows.reshape(batch, seq, d_model)
    y_sample = jnp.swapaxes(y_sample_rows.reshape(steps, nb, d_model), 0, 1)
    return (y_prompt, y_sample, jnp.stack(conv_p_out), jnp.stack(pool_p_out),
            jnp.stack(conv_s_out), jnp.stack(pool_s_out), jnp.stack(v_s_out))
```

```python
import functools

import jax
import jax.numpy as jnp
from jax import lax
from jax.experimental import pallas as pl
from jax.experimental.pallas import tpu as pltpu

EPS = 1e-6
PAST_LEN = 16384
CHUNK = 128
N_GROUPS = 4
POOL_WINDOWS = (2, 4, 8, 16)
POOL_BUF = max(POOL_WINDOWS) - 1
CONV_W = 3

SUBLANES = 8
LANES = 128
BF16_ROWS = 16
SKEW_PERIOD = 8
HALO = 16
FFN_COL_CHUNK = 512
ROW_TILE = 512
MXU_COLS = 256
VMEM_LIMIT_BYTES = 60 * 1024 * 1024

F32 = jnp.float32
BF16 = jnp.bfloat16


def _rms(x, g):
    return x * lax.rsqrt(jnp.mean(x * x, axis=-1, keepdims=True) + EPS) * g


def _layer_norm(x, g, b):
    xc = x - jnp.mean(x, axis=-1, keepdims=True)
    return xc * lax.rsqrt(jnp.mean(xc * xc, axis=-1, keepdims=True) + EPS) * g + b


def _dot(a, b):
    return jnp.dot(a, b, preferred_element_type=F32)


def _resident(block_shape, index_map):
    return pl.BlockSpec(block_shape, index_map, pipeline_mode=pl.Buffered(1))


def _whole(a, lead=()):
    nd = a.ndim - len(lead)
    return _resident((None,) * len(lead) + a.shape[len(lead):], lambda t: tuple(lead) + (0,) * nd)


def _skewed_cols(cols):
    return cols + LANES if cols % (SKEW_PERIOD * LANES) == 0 else cols


def _skew(w):
    pad = _skewed_cols(w.shape[-1]) - w.shape[-1]
    return jnp.pad(w, [(0, 0)] * (w.ndim - 1) + [(0, pad)]) if pad else w


class _Cast:
    def __init__(self, w, lead, n_prompt):
        rows, cols = w.shape[-2:]
        share = 1
        while n_prompt % share or rows % (n_prompt // share) or (rows // (n_prompt // share)) % BF16_ROWS:
            share *= 2
            assert share <= n_prompt
        n_blocks = n_prompt // share
        blk = rows // n_blocks

        def block(t):
            return jnp.minimum(t // share, n_blocks - 1)

        self.operand = w
        self.in_spec = pl.BlockSpec((None,) * len(lead) + (blk, cols), lambda t: tuple(lead) + (block(t), 0))
        self.out_spec = pl.BlockSpec((blk, _skewed_cols(cols)), lambda t: (block(t), 0))
        self.out_shape = jax.ShapeDtypeStruct((rows, _skewed_cols(cols)), BF16)


def _run_casts(cast_in, cast_out):
    for src, dst in zip(cast_in, cast_out):
        rows, cols = src.shape
        dst[:, :cols] = src[...].astype(BF16)
        if dst.shape[1] > cols:
            dst[:, cols:] = jnp.zeros((rows, dst.shape[1] - cols), BF16)


def _ffn_body(*refs, stages, two_in, two_out, n_cast, n_prompt):
    it = iter(refs)
    xp_ref = next(it)
    xs_ref = next(it) if two_in else None
    ng_ref = next(it)
    weights = [(next(it), next(it)) for _ in stages]
    cast_in = [next(it) for _ in range(n_cast)]
    op_ref = next(it)
    os_ref = next(it) if two_out else None
    cast_out = [next(it) for _ in range(n_cast)]

    t = pl.program_id(0)
    x = xp_ref[...]
    if two_in:
        x = jnp.where(t < n_prompt, x, xs_ref[...])
    for (layer, which), (wgu_ref, wdown_ref) in zip(stages, weights):
        d_ff = wdown_ref.shape[0]
        ng = ng_ref[layer]
        h = _rms(x, ng[4 * which:4 * which + 1]).astype(BF16)
        acc = None
        for c0 in range(0, d_ff, FFN_COL_CHUNK):
            c1 = min(c0 + FFN_COL_CHUNK, d_ff)
            g = _dot(h, wgu_ref[:, c0:c1])
            u = _dot(h, wgu_ref[:, d_ff + c0:d_ff + c1])
            a = (g * jax.nn.sigmoid(g) * u).astype(BF16)
            part = _dot(a, wdown_ref[c0:c1, :x.shape[1]])
            acc = part if acc is None else acc + part
        x = x + 0.5 * _rms(acc, ng[4 * which + 1:4 * which + 2])
    if two_out:
        @pl.when(t < n_prompt)
        def _():
            op_ref[...] = x

        @pl.when(t == n_prompt)
        def _():
            os_ref[...] = x
    else:
        op_ref[...] = x
    _run_casts(cast_in, cast_out)


def _ffn(xs_in, norm_g, stages, weights, casts, n_prompt, two_out):
    two_in = len(xs_in) == 2
    d = xs_in[0].shape[1]
    tile = pl.BlockSpec((ROW_TILE, d), lambda t: (t, 0))
    prompt_tile = pl.BlockSpec((ROW_TILE, d), lambda t: (jnp.minimum(t, n_prompt - 1), 0))
    sample_tile = pl.BlockSpec((ROW_TILE, d), lambda t: (0, 0))
    rows = (n_prompt + 1) * ROW_TILE
    body = functools.partial(_ffn_body, stages=tuple(stages), two_in=two_in, two_out=two_out,
                             n_cast=len(casts), n_prompt=n_prompt)
    if two_out:
        x_out_specs = [prompt_tile, sample_tile]
        x_out_shapes = [jax.ShapeDtypeStruct((n_prompt * ROW_TILE, d), F32),
                        jax.ShapeDtypeStruct((ROW_TILE, d), F32)]
    else:
        x_out_specs = [tile]
        x_out_shapes = [jax.ShapeDtypeStruct((rows, d), F32)]
    flat_weights = [w for pair in weights for w in pair]
    return pl.pallas_call(
        body,
        grid=(n_prompt + 1,),
        in_specs=([prompt_tile, sample_tile] if two_in else [tile])
        + [_whole(norm_g)] + [_whole(w) for w in flat_weights] + [c.in_spec for c in casts],
        out_specs=x_out_specs + [c.out_spec for c in casts],
        out_shape=x_out_shapes + [c.out_shape for c in casts],
        compiler_params=pltpu.CompilerParams(
            dimension_semantics=("arbitrary",), vmem_limit_bytes=VMEM_LIMIT_BYTES),
        name="ffn" + "".join(f"_l{layer}w{which}" for layer, which in stages),
    )(*xs_in, norm_g, *flat_weights, *[c.operand for c in casts])


def _project(h, win_ref, lo, hi):
    return _dot(h, win_ref[:, lo:hi])


def _join(parts):
    return jnp.concatenate(parts, axis=-1)


def _mixer_prompt_tile(j, is_last_j, x_ref, ng_ref, win_ref, ws_ref, bst_ref, lng_ref, lnb_ref, wconv_ref,
                       wpool_ref, pscale_ref, wa_ref, wb_ref, wc_ref, wo_ref,
                       o_ref, conv_out_ref, pool_out_ref, zext_ref, pext_ref, ya_ref, psum_ref,
                       *, ts, d_model, d_br):
    gw = d_br // N_GROUPS

    @pl.when(j == 0)
    def _():
        zext_ref[0:HALO, :] = jnp.zeros((HALO, d_br), F32)
        pext_ref[0:HALO, :] = jnp.zeros((HALO, d_br), F32)
        psum_ref[:, 0:SUBLANES, :] = jnp.zeros((N_GROUPS, SUBLANES, gw), F32)

    x = x_ref[...]
    ng = ng_ref[...]
    h = _rms(x, ng[2:3]).astype(BF16)

    def slabs(lo, hi):
        return [functools.partial(_project, h, win_ref, c, c + MXU_COLS) for c in range(lo, hi, MXU_COLS)]

    v_raw = _project(h, win_ref, d_br, 2 * d_br)
    u_raw = _project(h, win_ref, 0, d_br)
    u = jax.nn.gelu(u_raw)
    vb = _layer_norm(jax.nn.gelu(v_raw), lng_ref[...], lnb_ref[...]).astype(BF16)

    rows_i = lax.broadcasted_iota(jnp.int32, (CHUNK, CHUNK), 0)
    cols_i = lax.broadcasted_iota(jnp.int32, (CHUNK, CHUNK), 1)
    causal = cols_i <= rows_i
    n_chunks = ts // CHUNK

    def spatial(g):
        c0 = g * gw
        wsg = jnp.where(causal, ws_ref[g], 0.0).astype(BF16)
        mixed = _dot(wsg, _join([vb[n * CHUNK:(n + 1) * CHUNK, c0:c0 + gw] for n in range(n_chunks)]))
        bias = bst_ref[:, g:g + 1]
        for n in range(n_chunks):
            r0 = n * CHUNK
            ya_ref[r0:r0 + CHUNK, c0:c0 + gw] = (u[r0:r0 + CHUNK, c0:c0 + gw]
                                                * (mixed[:, n * gw:(n + 1) * gw] + bias))

    big = slabs(2 * d_br, 6 * d_br + 2 * d_model)
    done = []
    for k, slab in enumerate(big):
        done.append(slab())
        if k % 2 == 1 and k // 2 < N_GROUPS:
            spatial(k // 2)
    per = d_br // MXU_COLS
    xb, gb, gc, p = (_join(done[i * per:(i + 1) * per]) for i in range(4))
    per_gate = d_model // MXU_COLS
    gate_a = jax.nn.sigmoid(_join(done[4 * per:4 * per + per_gate]))
    gate_b = jax.nn.sigmoid(_join(done[4 * per + per_gate:]))
    merged = gate_a * _dot(ya_ref[...].astype(BF16), wa_ref[:, :d_model])

    z = gc * xb
    zext_ref[HALO:HALO + ts, :] = z
    wconv = wconv_ref[...]
    conv = wconv[0:1] * zext_ref[HALO - 2:HALO - 2 + ts, :]
    conv = conv + wconv[1:2] * zext_ref[HALO - 1:HALO - 1 + ts, :]
    conv = conv + wconv[2:3] * z
    yb = (gb * conv).astype(BF16)

    pext_ref[HALO:HALO + ts, :] = p
    pos = j * ts + lax.broadcasted_iota(jnp.int32, (ts, gw), 0)
    d_groups = []
    for g, w in enumerate(POOL_WINDOWS):
        c0 = g * gw
        run = pext_ref[:, c0:c0 + gw]
        shift = 1
        while shift < w:
            psum_ref[g, SUBLANES:, :] = run
            run = run + psum_ref[g, SUBLANES - shift:SUBLANES - shift + HALO + ts, :]
            shift *= 2
        cnt = jnp.minimum(pos + 1, w).astype(F32)
        d_groups.append(run[HALO:] / cnt - p[:, c0:c0 + gw])
    gate_c_parts, yc_parts = [], []
    zero_block = jnp.zeros((gw, gw), BF16)
    for k, slab in enumerate(slabs(6 * d_br + 2 * d_model, 6 * d_br + 3 * d_model)):
        gate_c_parts.append(slab())
        if k % 2 == 0:
            w_pair = jnp.concatenate([_join([wpool_ref[k], zero_block]),
                                      _join([zero_block, wpool_ref[k + 1]])], axis=0)
            yc_parts.append(_dot(_join(d_groups[k:k + 2]).astype(BF16), w_pair))
    yc = _join(yc_parts) * pscale_ref[...]

    merged = merged + gate_b * _dot(yb, wb_ref[:, :d_model])
    merged = merged + jax.nn.sigmoid(_join(gate_c_parts)) * _dot(yc.astype(BF16), wc_ref[:, :d_model])

    out = _dot(merged.astype(BF16), wo_ref[:, :d_model])
    o_ref[...] = x + _rms(out, ng[3:4])

    @pl.when(is_last_j)
    def _():
        conv_out_ref[...] = zext_ref[HALO + ts - (CONV_W - 1):HALO + ts, :]
        pool_out_ref[...] = pext_ref[HALO + ts - POOL_BUF:HALO + ts, :]

    zext_ref[0:HALO, :] = zext_ref[ts:ts + HALO, :]
    pext_ref[0:HALO, :] = pext_ref[ts:ts + HALO, :]


def _mixer_sample_tile(ws_ref, bs_ref, x_ref, conv_ref, pool_ref, ng_ref, win_ref, lng_ref, lnb_ref,
                       wconv_ref, wpool_ref, pscale_ref, wa_ref, wb_ref, wc_ref, wo_ref,
                       o_ref, v_out_ref, z_out_ref, p_out_ref,
                       *, layer, nb, steps, d_model, d_br, start_pos):
    gw = d_br // N_GROUPS
    x = x_ref[...]
    ng = ng_ref[...]
    h = _rms(x, ng[2:3]).astype(BF16)

    def blk(a, i):
        return a[i * nb:(i + 1) * nb]

    u = jax.nn.gelu(_project(h, win_ref, 0, d_br))
    v = _layer_norm(jax.nn.gelu(_project(h, win_ref, d_br, 2 * d_br)), lng_ref[...], lnb_ref[...])
    v_out_ref[...] = v
    ya_rows = []
    for t in range(steps):
        cols = []
        for g in range(N_GROUPS):
            c0 = g * gw
            mixed = None
            for s in range(t + 1):
                term = ws_ref[layer, g * steps * steps + t * steps + s] * blk(v, s)[:, c0:c0 + gw]
                mixed = term if mixed is None else mixed + term
            cols.append(mixed + bs_ref[layer, g * steps + t])
        ya_rows.append(blk(u, t) * _join(cols))
    ya = jnp.concatenate(ya_rows, axis=0)

    z = _project(h, win_ref, 4 * d_br, 5 * d_br) * _project(h, win_ref, 2 * d_br, 3 * d_br)
    z_out_ref[...] = z
    z_ext = [conv_ref[k] for k in range(CONV_W - 1)] + [blk(z, i) for i in range(steps)]
    wconv = wconv_ref[...]
    conv_rows = []
    for i in range(steps):
        c = wconv[0:1] * z_ext[i]
        for k in range(1, CONV_W):
            c = c + wconv[k:k + 1] * z_ext[i + k]
        conv_rows.append(c)
    yb = _project(h, win_ref, 3 * d_br, 4 * d_br) * jnp.concatenate(conv_rows, axis=0)

    p = _project(h, win_ref, 5 * d_br, 6 * d_br)
    p_out_ref[...] = p
    p_ext = [pool_ref[k] for k in range(POOL_BUF)] + [blk(p, i) for i in range(steps)]
    d_groups = []
    for g, w in enumerate(POOL_WINDOWS):
        c0 = g * gw
        rows_g = []
        for i in range(steps):
            s = p_ext[POOL_BUF + i][:, c0:c0 + gw]
            for k in range(1, w):
                s = s + p_ext[POOL_BUF + i - k][:, c0:c0 + gw]
            cnt = float(min(start_pos + i + 1, w))
            rows_g.append(s / cnt - p_ext[POOL_BUF + i][:, c0:c0 + gw])
        d_groups.append(jnp.concatenate(rows_g, axis=0))
    yc = _join([_dot(d_groups[g].astype(BF16), wpool_ref[g]) for g in range(N_GROUPS)]) * pscale_ref[...]

    merged = None
    for k, (y, w_ref) in enumerate(((ya, wa_ref), (yb, wb_ref), (yc, wc_ref))):
        lo = 6 * d_br + k * d_model
        term = jax.nn.sigmoid(_project(h, win_ref, lo, lo + d_model)) * _dot(y.astype(BF16), w_ref[:, :d_model])
        merged = term if merged is None else merged + term
    out = _dot(merged.astype(BF16), wo_ref[:, :d_model])
    o_ref[...] = x + _rms(out, ng[3:4])


def _mixer_body(*refs, layer, n_prompt, nj, n_cast, ts, nb, steps, d_model, d_br):
    it = iter(refs)
    ws_sm, bs_sm, x_ref, conv_s_ref, pool_s_ref = (next(it) for _ in range(5))
    (ng_ref, win_ref, ws_ref, bst_ref, lng_ref, lnb_ref, wconv_ref, wpool_ref, pscale_ref,
     wa_ref, wb_ref, wc_ref, wo_ref) = (next(it) for _ in range(13))
    cast_in = [next(it) for _ in range(n_cast)]
    o_ref, conv_out_ref, pool_out_ref, v_out_ref, z_out_ref, p_out_ref = (next(it) for _ in range(6))
    cast_out = [next(it) for _ in range(n_cast)]
    zext_ref, pext_ref, ya_ref, psum_ref = next(it), next(it), next(it), next(it)

    t = pl.program_id(0)

    @pl.when(t < n_prompt)
    def _():
        j = lax.rem(t, nj)
        _mixer_prompt_tile(j, j == nj - 1, x_ref, ng_ref, win_ref, ws_ref, bst_ref, lng_ref, lnb_ref,
                           wconv_ref, wpool_ref, pscale_ref, wa_ref, wb_ref, wc_ref, wo_ref,
                           o_ref, conv_out_ref, pool_out_ref, zext_ref, pext_ref, ya_ref, psum_ref,
                           ts=ts, d_model=d_model, d_br=d_br)

    @pl.when(t == n_prompt)
    def _():
        _mixer_sample_tile(ws_sm, bs_sm, x_ref, conv_s_ref, pool_s_ref, ng_ref, win_ref, lng_ref, lnb_ref,
                           wconv_ref, wpool_ref, pscale_ref, wa_ref, wb_ref, wc_ref, wo_ref,
                           o_ref, v_out_ref, z_out_ref, p_out_ref,
                           layer=layer, nb=nb, steps=steps, d_model=d_model, d_br=d_br, start_pos=PAST_LEN)

    _run_casts(cast_in, cast_out)


def _mixer(x, batch, steps, conv_s, pool_s, ws_small, bs_small, norm_g, win, w_s, b_s_t, ln_g, ln_b,
           w_conv, wpool, pool_scale, wa, wb, wc, wo, casts, layer, n_prompt):
    rows, d_model = x.shape
    d_br = wa.shape[1]
    ts = ROW_TILE
    nj = n_prompt // batch
    nb = ROW_TILE // steps
    assert n_prompt % batch == 0 and ts % CHUNK == 0 and ts >= HALO
    assert steps <= CHUNK and nb * steps == ROW_TILE and nb % SUBLANES == 0
    assert d_br % MXU_COLS == 0 and d_model // MXU_COLS == N_GROUPS
    body = functools.partial(_mixer_body, layer=layer, n_prompt=n_prompt, nj=nj, n_cast=len(casts),
                             ts=ts, nb=nb, steps=steps, d_model=d_model, d_br=d_br)
    smem = pl.BlockSpec(memory_space=pltpu.SMEM)
    tile = pl.BlockSpec((ts, d_model), lambda t: (t, 0))
    seq_of = lambda t: jnp.minimum(t // nj, batch - 1)
    sample_rows = pl.BlockSpec((ROW_TILE, d_br), lambda t: (0, 0))
    layer_ops = (norm_g, w_s, b_s_t, ln_g, ln_b, w_conv, wpool, pool_scale, wa, wb, wc, wo)
    return pl.pallas_call(
        body,
        grid=(n_prompt + 1,),
        in_specs=[smem, smem, tile, _whole(conv_s, (layer,)), _whole(pool_s, (layer,)),
                  _whole(norm_g, (layer,)), _whole(win)]
        + [_whole(a, (layer,)) for a in layer_ops[1:]] + [c.in_spec for c in casts],
        out_specs=[
            tile,
            pl.BlockSpec((None, CONV_W - 1, d_br), lambda t: (seq_of(t), 0, 0)),
            pl.BlockSpec((None, POOL_BUF, d_br), lambda t: (seq_of(t), 0, 0)),
            sample_rows, sample_rows, sample_rows,
        ] + [c.out_spec for c in casts],
        out_shape=[
            jax.ShapeDtypeStruct((rows, d_model), F32),
            jax.ShapeDtypeStruct((batch, CONV_W - 1, d_br), F32),
            jax.ShapeDtypeStruct((batch, POOL_BUF, d_br), F32),
            jax.ShapeDtypeStruct((ROW_TILE, d_br), F32),
            jax.ShapeDtypeStruct((ROW_TILE, d_br), F32),
            jax.ShapeDtypeStruct((ROW_TILE, d_br), F32),
        ] + [c.out_shape for c in casts],
        scratch_shapes=[
            pltpu.VMEM((HALO + ts, d_br), F32),
            pltpu.VMEM((HALO + ts, d_br), F32),
            pltpu.VMEM((ts, d_br), F32),
            pltpu.VMEM((N_GROUPS, SUBLANES + HALO + ts, d_br // N_GROUPS), F32),
        ],
        compiler_params=pltpu.CompilerParams(
            dimension_semantics=("arbitrary",), vmem_limit_bytes=VMEM_LIMIT_BYTES),
        name=f"mixer_l{layer}",
    )(ws_small, bs_small, x, conv_s, pool_s, norm_g, win, *layer_ops[1:], *[c.operand for c in casts])


def kernel(x_prompt, x_sample, state_conv, state_pool, norm_g, w_ffn_gu, w_ffn_down, w_in, w_s, b_s,
           ln_g, ln_b, w_conv, w_pool, pool_scale, w_a_out, w_b_out, w_c_out, w_o):
    depth = w_in.shape[0]
    batch, seq, d_model = x_prompt.shape
    nb, steps, _ = x_sample.shape
    assert (batch * seq) % ROW_TILE == 0 and nb * steps == ROW_TILE
    n_prompt = batch * seq // ROW_TILE

    wpool = w_pool.astype(BF16)
    wa, wb, wc, wo = (_skew(w.astype(BF16)) for w in (w_a_out, w_b_out, w_c_out, w_o))
    lng, lnb, pscale = (a[:, None, :] for a in (ln_g, ln_b, pool_scale))
    b_s_t = jnp.swapaxes(b_s, 1, 2)
    ws_small = w_s[:, :, :steps, :steps].reshape(depth, -1)
    bs_small = b_s[:, :, :steps].reshape(depth, -1)
    conv_s = jnp.swapaxes(state_conv, 1, 2)
    pool_s = jnp.swapaxes(state_pool, 1, 2)

    x_in = (x_prompt.reshape(batch * seq, d_model),
            jnp.swapaxes(x_sample, 0, 1).reshape(steps * nb, d_model))
    w_first = (_skew(w_ffn_gu[0, 0].astype(BF16)), _skew(w_ffn_down[0, 0].astype(BF16)))
    x, win = _ffn(x_in, norm_g, [(0, 0)], [w_first], [_Cast(w_in, (0,), n_prompt)], n_prompt, False)

    conv_p_out, pool_p_out, conv_s_out, pool_s_out, v_s_out = [], [], [], [], []
    for l in range(depth):
        last = l + 1 == depth
        following = [(l, 1)] if last else [(l, 1), (l + 1, 0)]
        casts = [_Cast(w, lw, n_prompt) for lw in following for w in (w_ffn_gu, w_ffn_down)]
        x, cp, pp, vs, zs, ps, *converted = _mixer(
            x, batch, steps, conv_s, pool_s, ws_small, bs_small, norm_g, win, w_s, b_s_t, lng, lnb,
            w_conv, wpool, pscale, wa, wb, wc, wo, casts, l, n_prompt)
        weights = [tuple(converted[2 * i:2 * i + 2]) for i in range(len(following))]
        if last:
            y_prompt_rows, y_sample_rows = _ffn((x,), norm_g, following, weights, [], n_prompt, True)
        else:
            x, win = _ffn((x,), norm_g, following, weights, [_Cast(w_in, (l + 1,), n_prompt)],
                          n_prompt, False)
        conv_p_out.append(cp)
        pool_p_out.append(pp)
        z_ext = jnp.concatenate([conv_s[l], zs.reshape(steps, nb, -1)], axis=0)[-(CONV_W - 1):]
        p_ext = jnp.concatenate([pool_s[l], ps.reshape(steps, nb, -1)], axis=0)[-POOL_BUF:]
        conv_s_out.append(jnp.swapaxes(z_ext, 0, 1))
        pool_s_out.append(jnp.swapaxes(p_ext, 0, 1))
        v_s_out.append(jnp.swapaxes(vs.reshape(steps, nb, -1), 0, 1))

    y_prompt = y_prompt_rows.reshape(batch, seq, d_model)
    y_sample = jnp.swapaxes(y_sample_rows.reshape(steps, nb, d_model), 0, 1)
    return (y_prompt, y_sample, jnp.stack(conv_p_out), jnp.stack(pool_p_out),
            jnp.stack(conv_s_out), jnp.stack(pool_s_out), jnp.stack(v_s_out))
```

```python
import functools

import jax
import jax.numpy as jnp
from jax import lax
from jax.experimental import pallas as pl
from jax.experimental.pallas import tpu as pltpu

EPS = 1e-6
PAST_LEN = 16384
CHUNK = 128
N_GROUPS = 4
POOL_WINDOWS = (2, 4, 8, 16)
POOL_BUF = max(POOL_WINDOWS) - 1
CONV_W = 3

SUBLANES = 8
LANES = 128
BF16_ROWS = 16
SKEW_PERIOD = 8
HALO = 16
FFN_COL_CHUNK = 1024
ROW_TILE = 512
MXU_COLS = 256
VMEM_LIMIT_BYTES = 60 * 1024 * 1024

F32 = jnp.float32
BF16 = jnp.bfloat16


def _rms(x, g):
    return x * lax.rsqrt(jnp.mean(x * x, axis=-1, keepdims=True) + EPS) * g


def _layer_norm(x, g, b):
    xc = x - jnp.mean(x, axis=-1, keepdims=True)
    return xc * lax.rsqrt(jnp.mean(xc * xc, axis=-1, keepdims=True) + EPS) * g + b


def _dot(a, b):
    return jnp.dot(a, b, preferred_element_type=F32)


def _resident(block_shape, index_map):
    return pl.BlockSpec(block_shape, index_map, pipeline_mode=pl.Buffered(1))


def _whole(a, lead=()):
    nd = a.ndim - len(lead)
    return _resident((None,) * len(lead) + a.shape[len(lead):], lambda t: tuple(lead) + (0,) * nd)


def _skewed_cols(cols):
    return cols + LANES if cols % (SKEW_PERIOD * LANES) == 0 else cols


def _skew(w):
    pad = _skewed_cols(w.shape[-1]) - w.shape[-1]
    return jnp.pad(w, [(0, 0)] * (w.ndim - 1) + [(0, pad)]) if pad else w


class _Cast:
    def __init__(self, w, lead, n_prompt):
        rows, cols = w.shape[-2:]
        share = 1
        while n_prompt % share or rows % (n_prompt // share) or (rows // (n_prompt // share)) % BF16_ROWS:
            share *= 2
            assert share <= n_prompt
        n_blocks = n_prompt // share
        blk = rows // n_blocks

        def block(t):
            return jnp.minimum(t // share, n_blocks - 1)

        self.operand = w
        self.in_spec = pl.BlockSpec((None,) * len(lead) + (blk, cols), lambda t: tuple(lead) + (block(t), 0))
        self.out_spec = pl.BlockSpec((blk, _skewed_cols(cols)), lambda t: (block(t), 0))
        self.out_shape = jax.ShapeDtypeStruct((rows, _skewed_cols(cols)), BF16)


def _run_casts(cast_in, cast_out):
    for src, dst in zip(cast_in, cast_out):
        rows, cols = src.shape
        dst[:, :cols] = src[...].astype(BF16)
        if dst.shape[1] > cols:
            dst[:, cols:] = jnp.zeros((rows, dst.shape[1] - cols), BF16)


def _ffn_body(*refs, stages, two_in, two_out, n_cast, n_prompt):
    it = iter(refs)
    xp_ref = next(it)
    xs_ref = next(it) if two_in else None
    ng_ref = next(it)
    weights = [(next(it), next(it)) for _ in stages]
    cast_in = [next(it) for _ in range(n_cast)]
    op_ref = next(it)
    os_ref = next(it) if two_out else None
    cast_out = [next(it) for _ in range(n_cast)]

    t = pl.program_id(0)
    x = xp_ref[...]
    if two_in:
        x = jnp.where(t < n_prompt, x, xs_ref[...])
    for (layer, which), (wgu_ref, wdown_ref) in zip(stages, weights):
        d_ff = wdown_ref.shape[0]
        ng = ng_ref[layer]
        h = _rms(x, ng[4 * which:4 * which + 1]).astype(BF16)
        acc = None
        for c0 in range(0, d_ff, FFN_COL_CHUNK):
            c1 = min(c0 + FFN_COL_CHUNK, d_ff)
            g = _dot(h, wgu_ref[:, c0:c1])
            u = _dot(h, wgu_ref[:, d_ff + c0:d_ff + c1])
            a = (g * jax.nn.sigmoid(g) * u).astype(BF16)
            part = _dot(a, wdown_ref[c0:c1, :x.shape[1]])
            acc = part if acc is None else acc + part
        x = x + _rms(acc, 0.5 * ng[4 * which + 1:4 * which + 2])
    if two_out:
        @pl.when(t < n_prompt)
        def _():
            op_ref[...] = x

        @pl.when(t == n_prompt)
        def _():
            os_ref[...] = x
    else:
        op_ref[...] = x
    _run_casts(cast_in, cast_out)


def _ffn(xs_in, norm_g, stages, weights, casts, n_prompt, two_out):
    two_in = len(xs_in) == 2
    d = xs_in[0].shape[1]
    tile = pl.BlockSpec((ROW_TILE, d), lambda t: (t, 0))
    prompt_tile = pl.BlockSpec((ROW_TILE, d), lambda t: (jnp.minimum(t, n_prompt - 1), 0))
    sample_tile = pl.BlockSpec((ROW_TILE, d), lambda t: (0, 0))
    rows = (n_prompt + 1) * ROW_TILE
    body = functools.partial(_ffn_body, stages=tuple(stages), two_in=two_in, two_out=two_out,
                             n_cast=len(casts), n_prompt=n_prompt)
    if two_out:
        x_out_specs = [prompt_tile, sample_tile]
        x_out_shapes = [jax.ShapeDtypeStruct((n_prompt * ROW_TILE, d), F32),
                        jax.ShapeDtypeStruct((ROW_TILE, d), F32)]
    else:
        x_out_specs = [tile]
        x_out_shapes = [jax.ShapeDtypeStruct((rows, d), F32)]
    flat_weights = [w for pair in weights for w in pair]
    return pl.pallas_call(
        body,
        grid=(n_prompt + 1,),
        in_specs=([prompt_tile, sample_tile] if two_in else [tile])
        + [_whole(norm_g)] + [_whole(w) for w in flat_weights] + [c.in_spec for c in casts],
        out_specs=x_out_specs + [c.out_spec for c in casts],
        out_shape=x_out_shapes + [c.out_shape for c in casts],
        compiler_params=pltpu.CompilerParams(
            dimension_semantics=("arbitrary",), vmem_limit_bytes=VMEM_LIMIT_BYTES),
        name="ffn" + "".join(f"_l{layer}w{which}" for layer, which in stages),
    )(*xs_in, norm_g, *flat_weights, *[c.operand for c in casts])


def _project(h, win_ref, lo, hi):
    return _dot(h, win_ref[:, lo:hi])


def _join(parts):
    return jnp.concatenate(parts, axis=-1)


def _mixer_prompt_tile(j, is_last_j, x_ref, ng_ref, win_ref, ws_ref, bst_ref, lng_ref, lnb_ref, wconv_ref,
                       wpool_ref, pscale_ref, wa_ref, wb_ref, wc_ref, wo_ref,
                       o_ref, conv_out_ref, pool_out_ref, zext_ref, pext_ref, ya_ref, psum_ref,
                       *, ts, d_model, d_br):
    gw = d_br // N_GROUPS

    @pl.when(j == 0)
    def _():
        zext_ref[0:HALO, :] = jnp.zeros((HALO, d_br), F32)
        pext_ref[0:HALO, :] = jnp.zeros((HALO, d_br), F32)
        psum_ref[:, 0:SUBLANES, :] = jnp.zeros((N_GROUPS, SUBLANES, gw), F32)

    x = x_ref[...]
    ng = ng_ref[...]
    h = _rms(x, ng[2:3]).astype(BF16)

    def slabs(lo, hi):
        return [functools.partial(_project, h, win_ref, c, c + MXU_COLS) for c in range(lo, hi, MXU_COLS)]

    v_raw = _project(h, win_ref, d_br, 2 * d_br)
    u_raw = _project(h, win_ref, 0, d_br)
    u = jax.nn.gelu(u_raw)
    vb = _layer_norm(jax.nn.gelu(v_raw), lng_ref[...], lnb_ref[...]).astype(BF16)

    rows_i = lax.broadcasted_iota(jnp.int32, (CHUNK, CHUNK), 0)
    cols_i = lax.broadcasted_iota(jnp.int32, (CHUNK, CHUNK), 1)
    causal = cols_i <= rows_i
    n_chunks = ts // CHUNK

    def spatial(g):
        c0 = g * gw
        wsg = jnp.where(causal, ws_ref[g], 0.0).astype(BF16)
        mixed = _dot(wsg, _join([vb[n * CHUNK:(n + 1) * CHUNK, c0:c0 + gw] for n in range(n_chunks)]))
        bias = bst_ref[:, g:g + 1]
        for n in range(n_chunks):
            r0 = n * CHUNK
            ya_ref[r0:r0 + CHUNK, c0:c0 + gw] = (u[r0:r0 + CHUNK, c0:c0 + gw]
                                                * (mixed[:, n * gw:(n + 1) * gw] + bias))

    big = slabs(2 * d_br, 6 * d_br + 2 * d_model)
    done = []
    for k, slab in enumerate(big):
        done.append(slab())
        if k % 2 == 1 and k // 2 < N_GROUPS:
            spatial(k // 2)
    per = d_br // MXU_COLS
    xb, gb, gc, p = (_join(done[i * per:(i + 1) * per]) for i in range(4))
    per_gate = d_model // MXU_COLS
    gate_a = jax.nn.sigmoid(_join(done[4 * per:4 * per + per_gate]))
    gate_b = jax.nn.sigmoid(_join(done[4 * per + per_gate:]))
    merged = gate_a * _dot(ya_ref[...].astype(BF16), wa_ref[:, :d_model])

    z = gc * xb
    zext_ref[HALO:HALO + ts, :] = z
    wconv = wconv_ref[...]
    conv = wconv[0:1] * zext_ref[HALO - 2:HALO - 2 + ts, :]
    conv = conv + wconv[1:2] * zext_ref[HALO - 1:HALO - 1 + ts, :]
    conv = conv + wconv[2:3] * z
    yb = (gb * conv).astype(BF16)

    pext_ref[HALO:HALO + ts, :] = p
    pos = j * ts + lax.broadcasted_iota(jnp.int32, (ts, gw), 0)
    d_groups = []
    for g, w in enumerate(POOL_WINDOWS):
        c0 = g * gw
        run = pext_ref[:, c0:c0 + gw]
        shift = 1
        while shift < w:
            psum_ref[g, SUBLANES:, :] = run
            run = run + psum_ref[g, SUBLANES - shift:SUBLANES - shift + HALO + ts, :]
            shift *= 2
        cnt = jnp.minimum(pos + 1, w).astype(F32)
        d_groups.append(run[HALO:] / cnt - p[:, c0:c0 + gw])
    gate_c_parts, yc_parts = [], []
    zero_block = jnp.zeros((gw, gw), BF16)
    for k, slab in enumerate(slabs(6 * d_br + 2 * d_model, 6 * d_br + 3 * d_model)):
        gate_c_parts.append(slab())
        if k % 2 == 0:
            w_pair = jnp.concatenate([_join([wpool_ref[k], zero_block]),
                                      _join([zero_block, wpool_ref[k + 1]])], axis=0)
            yc_parts.append(_dot(_join(d_groups[k:k + 2]).astype(BF16), w_pair))
    yc = _join(yc_parts) * pscale_ref[...]

    merged = merged + gate_b * _dot(yb, wb_ref[:, :d_model])
    merged = merged + jax.nn.sigmoid(_join(gate_c_parts)) * _dot(yc.astype(BF16), wc_ref[:, :d_model])

    out = _dot(merged.astype(BF16), wo_ref[:, :d_model])
    o_ref[...] = x + _rms(out, ng[3:4])

    @pl.when(is_last_j)
    def _():
        conv_out_ref[...] = zext_ref[HALO + ts - (CONV_W - 1):HALO + ts, :]
        pool_out_ref[...] = pext_ref[HALO + ts - POOL_BUF:HALO + ts, :]

    zext_ref[0:HALO, :] = zext_ref[ts:ts + HALO, :]
    pext_ref[0:HALO, :] = pext_ref[ts:ts + HALO, :]


def _mixer_sample_tile(ws_ref, bs_ref, x_ref, conv_ref, pool_ref, ng_ref, win_ref, lng_ref, lnb_ref,
                       wconv_ref, wpool_ref, pscale_ref, wa_ref, wb_ref, wc_ref, wo_ref,
                       o_ref, v_out_ref, z_out_ref, p_out_ref,
                       *, layer, nb, steps, d_model, d_br, start_pos):
    gw = d_br // N_GROUPS
    x = x_ref[...]
    ng = ng_ref[...]
    h = _rms(x, ng[2:3]).astype(BF16)

    def blk(a, i):
        return a[i * nb:(i + 1) * nb]

    u = jax.nn.gelu(_project(h, win_ref, 0, d_br))
    v = _layer_norm(jax.nn.gelu(_project(h, win_ref, d_br, 2 * d_br)), lng_ref[...], lnb_ref[...])
    v_out_ref[...] = v
    ya_rows = []
    for t in range(steps):
        cols = []
        for g in range(N_GROUPS):
            c0 = g * gw
            mixed = None
            for s in range(t + 1):
                term = ws_ref[layer, g * steps * steps + t * steps + s] * blk(v, s)[:, c0:c0 + gw]
                mixed = term if mixed is None else mixed + term
            cols.append(mixed + bs_ref[layer, g * steps + t])
        ya_rows.append(blk(u, t) * _join(cols))
    ya = jnp.concatenate(ya_rows, axis=0)

    z = _project(h, win_ref, 4 * d_br, 5 * d_br) * _project(h, win_ref, 2 * d_br, 3 * d_br)
    z_out_ref[...] = z
    z_ext = [conv_ref[k] for k in range(CONV_W - 1)] + [blk(z, i) for i in range(steps)]
    wconv = wconv_ref[...]
    conv_rows = []
    for i in range(steps):
        c = wconv[0:1] * z_ext[i]
        for k in range(1, CONV_W):
            c = c + wconv[k:k + 1] * z_ext[i + k]
        conv_rows.append(c)
    yb = _project(h, win_ref, 3 * d_br, 4 * d_br) * jnp.concatenate(conv_rows, axis=0)

    p = _project(h, win_ref, 5 * d_br, 6 * d_br)
    p_out_ref[...] = p
    p_ext = [pool_ref[k] for k in range(POOL_BUF)] + [blk(p, i) for i in range(steps)]
    d_groups = []
    for g, w in enumerate(POOL_WINDOWS):
        c0 = g * gw
        rows_g = []
        for i in range(steps):
            s = p_ext[POOL_BUF + i][:, c0:c0 + gw]
            for k in range(1, w):
                s = s + p_ext[POOL_BUF + i - k][:, c0:c0 + gw]
            cnt = float(min(start_pos + i + 1, w))
            rows_g.append(s / cnt - p_ext[POOL_BUF + i][:, c0:c0 + gw])
        d_groups.append(jnp.concatenate(rows_g, axis=0))
    yc = _join([_dot(d_groups[g].astype(BF16), wpool_ref[g]) for g in range(N_GROUPS)]) * pscale_ref[...]

    merged = None
    for k, (y, w_ref) in enumerate(((ya, wa_ref), (yb, wb_ref), (yc, wc_ref))):
        lo = 6 * d_br + k * d_model
        term = jax.nn.sigmoid(_project(h, win_ref, lo, lo + d_model)) * _dot(y.astype(BF16), w_ref[:, :d_model])
        merged = term if merged is None else merged + term
    out = _dot(merged.astype(BF16), wo_ref[:, :d_model])
    o_ref[...] = x + _rms(out, ng[3:4])


def _mixer_body(*refs, layer, n_prompt, nj, n_cast, ts, nb, steps, d_model, d_br):
    it = iter(refs)
    ws_sm, bs_sm, x_ref, conv_s_ref, pool_s_ref = (next(it) for _ in range(5))
    (ng_ref, win_ref, ws_ref, bst_ref, lng_ref, lnb_ref, wconv_ref, wpool_ref, pscale_ref,
     wa_ref, wb_ref, wc_ref, wo_ref) = (next(it) for _ in range(13))
    cast_in = [next(it) for _ in range(n_cast)]
    o_ref, conv_out_ref, pool_out_ref, v_out_ref, z_out_ref, p_out_ref = (next(it) for _ in range(6))
    cast_out = [next(it) for _ in range(n_cast)]
    zext_ref, pext_ref, ya_ref, psum_ref = next(it), next(it), next(it), next(it)

    t = pl.program_id(0)

    @pl.when(t < n_prompt)
    def _():
        j = lax.rem(t, nj)
        _mixer_prompt_tile(j, j == nj - 1, x_ref, ng_ref, win_ref, ws_ref, bst_ref, lng_ref, lnb_ref,
                           wconv_ref, wpool_ref, pscale_ref, wa_ref, wb_ref, wc_ref, wo_ref,
                           o_ref, conv_out_ref, pool_out_ref, zext_ref, pext_ref, ya_ref, psum_ref,
                           ts=ts, d_model=d_model, d_br=d_br)

    @pl.when(t == n_prompt)
    def _():
        _mixer_sample_tile(ws_sm, bs_sm, x_ref, conv_s_ref, pool_s_ref, ng_ref, win_ref, lng_ref, lnb_ref,
                           wconv_ref, wpool_ref, pscale_ref, wa_ref, wb_ref, wc_ref, wo_ref,
                           o_ref, v_out_ref, z_out_ref, p_out_ref,
                           layer=layer, nb=nb, steps=steps, d_model=d_model, d_br=d_br, start_pos=PAST_LEN)

    _run_casts(cast_in, cast_out)


def _mixer(x, batch, steps, conv_s, pool_s, ws_small, bs_small, norm_g, win, w_s, b_s_t, ln_g, ln_b,
           w_conv, wpool, pool_scale, wa, wb, wc, wo, casts, layer, n_prompt):
    rows, d_model = x.shape
    d_br = wa.shape[1]
    ts = ROW_TILE
    nj = n_prompt // batch
    nb = ROW_TILE // steps
    assert n_prompt % batch == 0 and ts % CHUNK == 0 and ts >= HALO
    assert steps <= CHUNK and nb * steps == ROW_TILE and nb % SUBLANES == 0
    assert d_br % MXU_COLS == 0 and d_model // MXU_COLS == N_GROUPS
    body = functools.partial(_mixer_body, layer=layer, n_prompt=n_prompt, nj=nj, n_cast=len(casts),
                             ts=ts, nb=nb, steps=steps, d_model=d_model, d_br=d_br)
    smem = pl.BlockSpec(memory_space=pltpu.SMEM)
    tile = pl.BlockSpec((ts, d_model), lambda t: (t, 0))
    seq_of = lambda t: jnp.minimum(t // nj, batch - 1)
    sample_rows = pl.BlockSpec((ROW_TILE, d_br), lambda t: (0, 0))
    layer_ops = (norm_g, w_s, b_s_t, ln_g, ln_b, w_conv, wpool, pool_scale, wa, wb, wc, wo)
    return pl.pallas_call(
        body,
        grid=(n_prompt + 1,),
        in_specs=[smem, smem, tile, _whole(conv_s, (layer,)), _whole(pool_s, (layer,)),
                  _whole(norm_g, (layer,)), _whole(win)]
        + [_whole(a, (layer,)) for a in layer_ops[1:]] + [c.in_spec for c in casts],
        out_specs=[
            tile,
            pl.BlockSpec((None, CONV_W - 1, d_br), lambda t: (seq_of(t), 0, 0)),
            pl.BlockSpec((None, POOL_BUF, d_br), lambda t: (seq_of(t), 0, 0)),
            sample_rows, sample_rows, sample_rows,
        ] + [c.out_spec for c in casts],
        out_shape=[
            jax.ShapeDtypeStruct((rows, d_model), F32),
            jax.ShapeDtypeStruct((batch, CONV_W - 1, d_br), F32),
            jax.ShapeDtypeStruct((batch, POOL_BUF, d_br), F32),
            jax.ShapeDtypeStruct((ROW_TILE, d_br), F32),
            jax.ShapeDtypeStruct((ROW_TILE, d_br), F32),
            jax.ShapeDtypeStruct((ROW_TILE, d_br), F32),
        ] + [c.out_shape for c in casts],
        scratch_shapes=[
            pltpu.VMEM((HALO + ts, d_br), F32),
            pltpu.VMEM((HALO + ts, d_br), F32),
            pltpu.VMEM((ts, d_br), F32),
            pltpu.VMEM((N_GROUPS, SUBLANES + HALO + ts, d_br // N_GROUPS), F32),
        ],
        compiler_params=pltpu.CompilerParams(
            dimension_semantics=("arbitrary",), vmem_limit_bytes=VMEM_LIMIT_BYTES),
        name=f"mixer_l{layer}",
    )(ws_small, bs_small, x, conv_s, pool_s, norm_g, win, *layer_ops[1:], *[c.operand for c in casts])


def kernel(x_prompt, x_sample, state_conv, state_pool, norm_g, w_ffn_gu, w_ffn_down, w_in, w_s, b_s,
           ln_g, ln_b, w_conv, w_pool, pool_scale, w_a_out, w_b_out, w_c_out, w_o):
    depth = w_in.shape[0]
    batch, seq, d_model = x_prompt.shape
    nb, steps, _ = x_sample.shape
    assert (batch * seq) % ROW_TILE == 0 and nb * steps == ROW_TILE
    n_prompt = batch * seq // ROW_TILE

    wpool = w_pool.astype(BF16)
    wa, wb, wc, wo = (_skew(w.astype(BF16)) for w in (w_a_out, w_b_out, w_c_out, w_o))
    lng, lnb, pscale = (a[:, None, :] for a in (ln_g, ln_b, pool_scale))
    b_s_t = jnp.swapaxes(b_s, 1, 2)
    ws_small = w_s[:, :, :steps, :steps].reshape(depth, -1)
    bs_small = b_s[:, :, :steps].reshape(depth, -1)
    conv_s = jnp.swapaxes(state_conv, 1, 2)
    pool_s = jnp.swapaxes(state_pool, 1, 2)

    x_in = (x_prompt.reshape(batch * seq, d_model),
            jnp.swapaxes(x_sample, 0, 1).reshape(steps * nb, d_model))
    w_ffn = (_skew(w_ffn_gu[0, 0].astype(BF16)), _skew(w_ffn_down[0, 0].astype(BF16)))

    def ffn_casts(layer, which):
        return [_Cast(w_ffn_gu, (layer, which), n_prompt), _Cast(w_ffn_down, (layer, which), n_prompt)]

    conv_p_out, pool_p_out, conv_s_out, pool_s_out, v_s_out = [], [], [], [], []
    for l in range(depth):
        x, win = _ffn(x_in, norm_g, [(l, 0)], [w_ffn], [_Cast(w_in, (l,), n_prompt)], n_prompt, False)
        x, cp, pp, vs, zs, ps, *w_ffn = _mixer(
            x, batch, steps, conv_s, pool_s, ws_small, bs_small, norm_g, win, w_s, b_s_t, lng, lnb,
            w_conv, wpool, pscale, wa, wb, wc, wo, ffn_casts(l, 1), l, n_prompt)
        if l + 1 < depth:
            x, *w_ffn = _ffn((x,), norm_g, [(l, 1)], [w_ffn], ffn_casts(l + 1, 0), n_prompt, False)
            x_in = (x,)
        else:
            y_prompt_rows, y_sample_rows = _ffn((x,), norm_g, [(l, 1)], [w_ffn], [], n_prompt, True)
        conv_p_out.append(cp)
        pool_p_out.append(pp)
        z_ext = jnp.concatenate([conv_s[l], zs.reshape(steps, nb, -1)], axis=0)[-(CONV_W - 1):]
        p_ext = jnp.concatenate([pool_s[l], ps.reshape(steps, nb, -1)], axis=0)[-POOL_BUF:]
        conv_s_out.append(jnp.swapaxes(z_ext, 0, 1))
        pool_s_out.append(jnp.swapaxes(p_ext, 0, 1))
        v_s_out.append(jnp.swapaxes(vs.reshape(steps, nb, -1), 0, 1))

    y_prompt = y_prompt_rows.reshape(batch, seq, d_model)
    y_sample = jnp.swapaxes(y_sample_rows.reshape(steps, nb, d_model), 0, 1)
    return (y_prompt, y_sample, jnp.stack(conv_p_out), jnp.stack(pool_p_out),
            jnp.stack(conv_s_out), jnp.stack(pool_s_out), jnp.stack(v_s_out))
```

```python
import functools

import jax
import jax.numpy as jnp
from jax import lax
from jax.experimental import pallas as pl
from jax.experimental.pallas import tpu as pltpu

EPS = 1e-6
PAST_LEN = 16384
CHUNK = 128
N_GROUPS = 4
POOL_WINDOWS = (2, 4, 8, 16)
POOL_BUF = max(POOL_WINDOWS) - 1
CONV_W = 3

SUBLANES = 8
LANES = 128
BF16_ROWS = 16
SKEW_PERIOD = 8
HALO = 16
FFN_COL_CHUNK = 512
FFN_ROW_TILE_MAX = 1088
ROW_TILE = 512
MXU_COLS = 256
VMEM_LIMIT_BYTES = 60 * 1024 * 1024

F32 = jnp.float32
BF16 = jnp.bfloat16


def _rms(x, g):
    return x * lax.rsqrt(jnp.mean(x * x, axis=-1, keepdims=True) + EPS) * g


def _layer_norm(x, g, b):
    xc = x - jnp.mean(x, axis=-1, keepdims=True)
    return xc * lax.rsqrt(jnp.mean(xc * xc, axis=-1, keepdims=True) + EPS) * g + b


def _dot(a, b):
    return jnp.dot(a, b, preferred_element_type=F32)


def _resident(block_shape, index_map):
    return pl.BlockSpec(block_shape, index_map, pipeline_mode=pl.Buffered(1))


def _whole(a, lead=()):
    nd = a.ndim - len(lead)
    return _resident((None,) * len(lead) + a.shape[len(lead):], lambda t: tuple(lead) + (0,) * nd)


def _skewed_cols(cols):
    return cols + LANES if cols % (SKEW_PERIOD * LANES) == 0 else cols


def _skew(w):
    pad = _skewed_cols(w.shape[-1]) - w.shape[-1]
    return jnp.pad(w, [(0, 0)] * (w.ndim - 1) + [(0, pad)]) if pad else w


class _Cast:
    def __init__(self, w, lead, n_prompt):
        rows, cols = w.shape[-2:]
        share = 1
        while n_prompt % share or rows % (n_prompt // share) or (rows // (n_prompt // share)) % BF16_ROWS:
            share *= 2
            assert share <= n_prompt
        n_blocks = n_prompt // share
        blk = rows // n_blocks

        def block(t):
            return jnp.minimum(t // share, n_blocks - 1)

        self.operand = w
        self.in_spec = pl.BlockSpec((None,) * len(lead) + (blk, cols), lambda t: tuple(lead) + (block(t), 0))
        self.out_spec = pl.BlockSpec((blk, _skewed_cols(cols)), lambda t: (block(t), 0))
        self.out_shape = jax.ShapeDtypeStruct((rows, _skewed_cols(cols)), BF16)


def _run_casts(cast_in, cast_out):
    for src, dst in zip(cast_in, cast_out):
        rows, cols = src.shape
        dst[:, :cols] = src[...].astype(BF16)
        if dst.shape[1] > cols:
            dst[:, cols:] = jnp.zeros((rows, dst.shape[1] - cols), BF16)


def _ffn_body(*refs, stages, two_in, two_out, n_cast, n_prompt):
    it = iter(refs)
    xp_ref = next(it)
    xs_ref = next(it) if two_in else None
    ng_ref = next(it)
    weights = [(next(it), next(it)) for _ in stages]
    cast_in = [next(it) for _ in range(n_cast)]
    op_ref = next(it)
    os_ref = next(it) if two_out else None
    cast_out = [next(it) for _ in range(n_cast)]

    t = pl.program_id(0)
    x = xp_ref[...]
    if two_in:
        x = jnp.where(t < n_prompt, x, xs_ref[...])
    for (layer, which), (wgu_ref, wdown_ref) in zip(stages, weights):
        d_ff = wdown_ref.shape[0]
        ng = ng_ref[layer]
        h = _rms(x, ng[4 * which:4 * which + 1]).astype(BF16)
        acc = None
        for c0 in range(0, d_ff, FFN_COL_CHUNK):
            c1 = min(c0 + FFN_COL_CHUNK, d_ff)
            g = _dot(h, wgu_ref[:, c0:c1])
            u = _dot(h, wgu_ref[:, d_ff + c0:d_ff + c1])
            a = (g * jax.nn.sigmoid(g) * u).astype(BF16)
            part = _dot(a, wdown_ref[c0:c1, :x.shape[1]])
            acc = part if acc is None else acc + part
        x = x + _rms(acc, 0.5 * ng[4 * which + 1:4 * which + 2])
    if two_out:
        @pl.when(t < n_prompt)
        def _():
            op_ref[...] = x

        @pl.when(t == n_prompt)
        def _():
            os_ref[...] = x
    else:
        op_ref[...] = x
    _run_casts(cast_in, cast_out)


def _ffn_row_tile(rows):
    return max(r for r in range(BF16_ROWS, FFN_ROW_TILE_MAX + 1, BF16_ROWS) if rows % r == 0)


def _ffn(xs_in, norm_g, stages, weights, casts, n_prompt, two_out):
    two_in = len(xs_in) == 2
    d = xs_in[0].shape[1]
    rows = (n_prompt + 1) * ROW_TILE
    row_tile = ROW_TILE if two_in or two_out else _ffn_row_tile(rows)
    tile = pl.BlockSpec((row_tile, d), lambda t: (t, 0))
    prompt_tile = pl.BlockSpec((ROW_TILE, d), lambda t: (jnp.minimum(t, n_prompt - 1), 0))
    sample_tile = pl.BlockSpec((ROW_TILE, d), lambda t: (0, 0))
    body = functools.partial(_ffn_body, stages=tuple(stages), two_in=two_in, two_out=two_out,
                             n_cast=len(casts), n_prompt=n_prompt)
    if two_out:
        x_out_specs = [prompt_tile, sample_tile]
        x_out_shapes = [jax.ShapeDtypeStruct((n_prompt * ROW_TILE, d), F32),
                        jax.ShapeDtypeStruct((ROW_TILE, d), F32)]
    else:
        x_out_specs = [tile]
        x_out_shapes = [jax.ShapeDtypeStruct((rows, d), F32)]
    flat_weights = [w for pair in weights for w in pair]
    return pl.pallas_call(
        body,
        grid=(rows // row_tile,),
        in_specs=([prompt_tile, sample_tile] if two_in else [tile])
        + [_whole(norm_g)] + [_whole(w) for w in flat_weights] + [c.in_spec for c in casts],
        out_specs=x_out_specs + [c.out_spec for c in casts],
        out_shape=x_out_shapes + [c.out_shape for c in casts],
        compiler_params=pltpu.CompilerParams(
            dimension_semantics=("arbitrary",), vmem_limit_bytes=VMEM_LIMIT_BYTES),
        name="ffn" + "".join(f"_l{layer}w{which}" for layer, which in stages),
    )(*xs_in, norm_g, *flat_weights, *[c.operand for c in casts])


def _project(h, win_ref, lo, hi):
    return _dot(h, win_ref[:, lo:hi])


def _join(parts):
    return jnp.concatenate(parts, axis=-1)


def _mixer_prompt_tile(j, is_last_j, x_ref, ng_ref, win_ref, ws_ref, bst_ref, lng_ref, lnb_ref, wconv_ref,
                       wpool_ref, pscale_ref, wa_ref, wb_ref, wc_ref, wo_ref,
                       o_ref, conv_out_ref, pool_out_ref, zext_ref, pext_ref, ya_ref, psum_ref,
                       *, ts, d_model, d_br):
    gw = d_br // N_GROUPS

    @pl.when(j == 0)
    def _():
        zext_ref[0:HALO, :] = jnp.zeros((HALO, d_br), F32)
        pext_ref[0:HALO, :] = jnp.zeros((HALO, d_br), F32)
        psum_ref[:, 0:SUBLANES, :] = jnp.zeros((N_GROUPS, SUBLANES, gw), F32)

    x = x_ref[...]
    ng = ng_ref[...]
    h = _rms(x, ng[2:3]).astype(BF16)

    def slabs(lo, hi):
        return [functools.partial(_project, h, win_ref, c, c + MXU_COLS) for c in range(lo, hi, MXU_COLS)]

    v_raw = _project(h, win_ref, d_br, 2 * d_br)
    u_raw = _project(h, win_ref, 0, d_br)
    u = jax.nn.gelu(u_raw)
    vb = _layer_norm(jax.nn.gelu(v_raw), lng_ref[...], lnb_ref[...]).astype(BF16)

    rows_i = lax.broadcasted_iota(jnp.int32, (CHUNK, CHUNK), 0)
    cols_i = lax.broadcasted_iota(jnp.int32, (CHUNK, CHUNK), 1)
    causal = cols_i <= rows_i
    n_chunks = ts // CHUNK

    def spatial(g):
        c0 = g * gw
        wsg = jnp.where(causal, ws_ref[g], 0.0).astype(BF16)
        mixed = _dot(wsg, _join([vb[n * CHUNK:(n + 1) * CHUNK, c0:c0 + gw] for n in range(n_chunks)]))
        bias = bst_ref[:, g:g + 1]
        for n in range(n_chunks):
            r0 = n * CHUNK
            ya_ref[r0:r0 + CHUNK, c0:c0 + gw] = (u[r0:r0 + CHUNK, c0:c0 + gw]
                                                * (mixed[:, n * gw:(n + 1) * gw] + bias))

    big = slabs(2 * d_br, 6 * d_br + 2 * d_model)
    done = []
    for k, slab in enumerate(big):
        done.append(slab())
        if k % 2 == 1 and k // 2 < N_GROUPS:
            spatial(k // 2)
    per = d_br // MXU_COLS
    xb, gb, gc, p = (_join(done[i * per:(i + 1) * per]) for i in range(4))
    per_gate = d_model // MXU_COLS
    gate_a = jax.nn.sigmoid(_join(done[4 * per:4 * per + per_gate]))
    gate_b = jax.nn.sigmoid(_join(done[4 * per + per_gate:]))
    merged = gate_a * _dot(ya_ref[...].astype(BF16), wa_ref[:, :d_model])

    z = gc * xb
    zext_ref[HALO:HALO + ts, :] = z
    wconv = wconv_ref[...]
    conv = wconv[0:1] * zext_ref[HALO - 2:HALO - 2 + ts, :]
    conv = conv + wconv[1:2] * zext_ref[HALO - 1:HALO - 1 + ts, :]
    conv = conv + wconv[2:3] * z
    yb = (gb * conv).astype(BF16)

    pext_ref[HALO:HALO + ts, :] = p
    pos = j * ts + lax.broadcasted_iota(jnp.int32, (ts, gw), 0)
    d_groups = []
    for g, w in enumerate(POOL_WINDOWS):
        c0 = g * gw
        run = pext_ref[:, c0:c0 + gw]
        shift = 1
        while shift < w:
            psum_ref[g, SUBLANES:, :] = run
            run = run + psum_ref[g, SUBLANES - shift:SUBLANES - shift + HALO + ts, :]
            shift *= 2
        cnt = jnp.minimum(pos + 1, w).astype(F32)
        d_groups.append(run[HALO:] / cnt - p[:, c0:c0 + gw])
    gate_c_parts, yc_parts = [], []
    zero_block = jnp.zeros((gw, gw), BF16)
    for k, slab in enumerate(slabs(6 * d_br + 2 * d_model, 6 * d_br + 3 * d_model)):
        gate_c_parts.append(slab())
        if k % 2 == 0:
            w_pair = jnp.concatenate([_join([wpool_ref[k], zero_block]),
                                      _join([zero_block, wpool_ref[k + 1]])], axis=0)
            yc_parts.append(_dot(_join(d_groups[k:k + 2]).astype(BF16), w_pair))
    yc = _join(yc_parts) * pscale_ref[...]

    merged = merged + gate_b * _dot(yb, wb_ref[:, :d_model])
    merged = merged + jax.nn.sigmoid(_join(gate_c_parts)) * _dot(yc.astype(BF16), wc_ref[:, :d_model])

    out = _dot(merged.astype(BF16), wo_ref[:, :d_model])
    o_ref[...] = x + _rms(out, ng[3:4])

    @pl.when(is_last_j)
    def _():
        conv_out_ref[...] = zext_ref[HALO + ts - (CONV_W - 1):HALO + ts, :]
        pool_out_ref[...] = pext_ref[HALO + ts - POOL_BUF:HALO + ts, :]

    zext_ref[0:HALO, :] = zext_ref[ts:ts + HALO, :]
    pext_ref[0:HALO, :] = pext_ref[ts:ts + HALO, :]


def _mixer_sample_tile(ws_ref, bs_ref, x_ref, conv_ref, pool_ref, ng_ref, win_ref, lng_ref, lnb_ref,
                       wconv_ref, wpool_ref, pscale_ref, wa_ref, wb_ref, wc_ref, wo_ref,
                       o_ref, v_out_ref, z_out_ref, p_out_ref,
                       *, layer, nb, steps, d_model, d_br, start_pos):
    gw = d_br // N_GROUPS
    x = x_ref[...]
    ng = ng_ref[...]
    h = _rms(x, ng[2:3]).astype(BF16)

    def blk(a, i):
        return a[i * nb:(i + 1) * nb]

    u = jax.nn.gelu(_project(h, win_ref, 0, d_br))
    v = _layer_norm(jax.nn.gelu(_project(h, win_ref, d_br, 2 * d_br)), lng_ref[...], lnb_ref[...])
    v_out_ref[...] = v
    ya_rows = []
    for t in range(steps):
        cols = []
        for g in range(N_GROUPS):
            c0 = g * gw
            mixed = None
            for s in range(t + 1):
                term = ws_ref[layer, g * steps * steps + t * steps + s] * blk(v, s)[:, c0:c0 + gw]
                mixed = term if mixed is None else mixed + term
            cols.append(mixed + bs_ref[layer, g * steps + t])
        ya_rows.append(blk(u, t) * _join(cols))
    ya = jnp.concatenate(ya_rows, axis=0)

    z = _project(h, win_ref, 4 * d_br, 5 * d_br) * _project(h, win_ref, 2 * d_br, 3 * d_br)
    z_out_ref[...] = z
    z_ext = [conv_ref[k] for k in range(CONV_W - 1)] + [blk(z, i) for i in range(steps)]
    wconv = wconv_ref[...]
    conv_rows = []
    for i in range(steps):
        c = wconv[0:1] * z_ext[i]
        for k in range(1, CONV_W):
            c = c + wconv[k:k + 1] * z_ext[i + k]
        conv_rows.append(c)
    yb = _project(h, win_ref, 3 * d_br, 4 * d_br) * jnp.concatenate(conv_rows, axis=0)

    p = _project(h, win_ref, 5 * d_br, 6 * d_br)
    p_out_ref[...] = p
    p_ext = [pool_ref[k] for k in range(POOL_BUF)] + [blk(p, i) for i in range(steps)]
    d_groups = []
    for g, w in enumerate(POOL_WINDOWS):
        c0 = g * gw
        rows_g = []
        for i in range(steps):
            s = p_ext[POOL_BUF + i][:, c0:c0 + gw]
            for k in range(1, w):
                s = s + p_ext[POOL_BUF + i - k][:, c0:c0 + gw]
            cnt = float(min(start_pos + i + 1, w))
            rows_g.append(s / cnt - p_ext[POOL_BUF + i][:, c0:c0 + gw])
        d_groups.append(jnp.concatenate(rows_g, axis=0))
    yc = _join([_dot(d_groups[g].astype(BF16), wpool_ref[g]) for g in range(N_GROUPS)]) * pscale_ref[...]

    merged = None
    for k, (y, w_ref) in enumerate(((ya, wa_ref), (yb, wb_ref), (yc, wc_ref))):
        lo = 6 * d_br + k * d_model
        term = jax.nn.sigmoid(_project(h, win_ref, lo, lo + d_model)) * _dot(y.astype(BF16), w_ref[:, :d_model])
        merged = term if merged is None else merged + term
    out = _dot(merged.astype(BF16), wo_ref[:, :d_model])
    o_ref[...] = x + _rms(out, ng[3:4])


def _mixer_body(*refs, layer, n_prompt, nj, n_cast, ts, nb, steps, d_model, d_br):
    it = iter(refs)
    ws_sm, bs_sm, x_ref, conv_s_ref, pool_s_ref = (next(it) for _ in range(5))
    (ng_ref, win_ref, ws_ref, bst_ref, lng_ref, lnb_ref, wconv_ref, wpool_ref, pscale_ref,
     wa_ref, wb_ref, wc_ref, wo_ref) = (next(it) for _ in range(13))
    cast_in = [next(it) for _ in range(n_cast)]
    o_ref, conv_out_ref, pool_out_ref, v_out_ref, z_out_ref, p_out_ref = (next(it) for _ in range(6))
    cast_out = [next(it) for _ in range(n_cast)]
    zext_ref, pext_ref, ya_ref, psum_ref = next(it), next(it), next(it), next(it)

    t = pl.program_id(0)

    @pl.when(t < n_prompt)
    def _():
        j = lax.rem(t, nj)
        _mixer_prompt_tile(j, j == nj - 1, x_ref, ng_ref, win_ref, ws_ref, bst_ref, lng_ref, lnb_ref,
                           wconv_ref, wpool_ref, pscale_ref, wa_ref, wb_ref, wc_ref, wo_ref,
                           o_ref, conv_out_ref, pool_out_ref, zext_ref, pext_ref, ya_ref, psum_ref,
                           ts=ts, d_model=d_model, d_br=d_br)

    @pl.when(t == n_prompt)
    def _():
        _mixer_sample_tile(ws_sm, bs_sm, x_ref, conv_s_ref, pool_s_ref, ng_ref, win_ref, lng_ref, lnb_ref,
                           wconv_ref, wpool_ref, pscale_ref, wa_ref, wb_ref, wc_ref, wo_ref,
                           o_ref, v_out_ref, z_out_ref, p_out_ref,
                           layer=layer, nb=nb, steps=steps, d_model=d_model, d_br=d_br, start_pos=PAST_LEN)

    _run_casts(cast_in, cast_out)


def _mixer(x, batch, steps, conv_s, pool_s, ws_small, bs_small, norm_g, win, w_s, b_s_t, ln_g, ln_b,
           w_conv, wpool, pool_scale, wa, wb, wc, wo, casts, layer, n_prompt):
    rows, d_model = x.shape
    d_br = wa.shape[1]
    ts = ROW_TILE
    nj = n_prompt // batch
    nb = ROW_TILE // steps
    assert n_prompt % batch == 0 and ts % CHUNK == 0 and ts >= HALO
    assert steps <= CHUNK and nb * steps == ROW_TILE and nb % SUBLANES == 0
    assert d_br % MXU_COLS == 0 and d_model // MXU_COLS == N_GROUPS
    body = functools.partial(_mixer_body, layer=layer, n_prompt=n_prompt, nj=nj, n_cast=len(casts),
                             ts=ts, nb=nb, steps=steps, d_model=d_model, d_br=d_br)
    smem = pl.BlockSpec(memory_space=pltpu.SMEM)
    tile = pl.BlockSpec((ts, d_model), lambda t: (t, 0))
    seq_of = lambda t: jnp.minimum(t // nj, batch - 1)
    sample_rows = pl.BlockSpec((ROW_TILE, d_br), lambda t: (0, 0))
    layer_ops = (norm_g, w_s, b_s_t, ln_g, ln_b, w_conv, wpool, pool_scale, wa, wb, wc, wo)
    return pl.pallas_call(
        body,
        grid=(n_prompt + 1,),
        in_specs=[smem, smem, tile, _whole(conv_s, (layer,)), _whole(pool_s, (layer,)),
                  _whole(norm_g, (layer,)), _whole(win)]
        + [_whole(a, (layer,)) for a in layer_ops[1:]] + [c.in_spec for c in casts],
        out_specs=[
            tile,
            pl.BlockSpec((None, CONV_W - 1, d_br), lambda t: (seq_of(t), 0, 0)),
            pl.BlockSpec((None, POOL_BUF, d_br), lambda t: (seq_of(t), 0, 0)),
            sample_rows, sample_rows, sample_rows,
        ] + [c.out_spec for c in casts],
        out_shape=[
            jax.ShapeDtypeStruct((rows, d_model), F32),
            jax.ShapeDtypeStruct((batch, CONV_W - 1, d_br), F32),
            jax.ShapeDtypeStruct((batch, POOL_BUF, d_br), F32),
            jax.ShapeDtypeStruct((ROW_TILE, d_br), F32),
            jax.ShapeDtypeStruct((ROW_TILE, d_br), F32),
            jax.ShapeDtypeStruct((ROW_TILE, d_br), F32),
        ] + [c.out_shape for c in casts],
        scratch_shapes=[
            pltpu.VMEM((HALO + ts, d_br), F32),
            pltpu.VMEM((HALO + ts, d_br), F32),
            pltpu.VMEM((ts, d_br), F32),
            pltpu.VMEM((N_GROUPS, SUBLANES + HALO + ts, d_br // N_GROUPS), F32),
        ],
        compiler_params=pltpu.CompilerParams(
            dimension_semantics=("arbitrary",), vmem_limit_bytes=VMEM_LIMIT_BYTES),
        name=f"mixer_l{layer}",
    )(ws_small, bs_small, x, conv_s, pool_s, norm_g, win, *layer_ops[1:], *[c.operand for c in casts])


def kernel(x_prompt, x_sample, state_conv, state_pool, norm_g, w_ffn_gu, w_ffn_down, w_in, w_s, b_s,
           ln_g, ln_b, w_conv, w_pool, pool_scale, w_a_out, w_b_out, w_c_out, w_o):
    depth = w_in.shape[0]
    batch, seq, d_model = x_prompt.shape
    nb, steps, _ = x_sample.shape
    assert (batch * seq) % ROW_TILE == 0 and nb * steps == ROW_TILE
    n_prompt = batch * seq // ROW_TILE

    wpool = w_pool.astype(BF16)
    wa, wb, wc, wo = (_skew(w.astype(BF16)) for w in (w_a_out, w_b_out, w_c_out, w_o))
    lng, lnb, pscale = (a[:, None, :] for a in (ln_g, ln_b, pool_scale))
    b_s_t = jnp.swapaxes(b_s, 1, 2)
    ws_small = w_s[:, :, :steps, :steps].reshape(depth, -1)
    bs_small = b_s[:, :, :steps].reshape(depth, -1)
    conv_s = jnp.swapaxes(state_conv, 1, 2)
    pool_s = jnp.swapaxes(state_pool, 1, 2)

    x_in = (x_prompt.reshape(batch * seq, d_model),
            jnp.swapaxes(x_sample, 0, 1).reshape(steps * nb, d_model))
    w_ffn = (_skew(w_ffn_gu[0, 0].astype(BF16)), _skew(w_ffn_down[0, 0].astype(BF16)))

    rows = (n_prompt + 1) * ROW_TILE
    ffn_steps = rows // _ffn_row_tile(rows)

    def ffn_casts(layer, which, n_steps):
        return [_Cast(w_ffn_gu, (layer, which), n_steps), _Cast(w_ffn_down, (layer, which), n_steps)]

    conv_p_out, pool_p_out, conv_s_out, pool_s_out, v_s_out = [], [], [], [], []
    for l in range(depth):
        x, win = _ffn(x_in, norm_g, [(l, 0)], [w_ffn], [_Cast(w_in, (l,), ffn_steps if l else n_prompt)],
                      n_prompt, False)
        x, cp, pp, vs, zs, ps, *w_ffn = _mixer(
            x, batch, steps, conv_s, pool_s, ws_small, bs_small, norm_g, win, w_s, b_s_t, lng, lnb,
            w_conv, wpool, pscale, wa, wb, wc, wo, ffn_casts(l, 1, n_prompt), l, n_prompt)
        if l + 1 < depth:
            x, *w_ffn = _ffn((x,), norm_g, [(l, 1)], [w_ffn], ffn_casts(l + 1, 0, ffn_steps), n_prompt, False)
            x_in = (x,)
        else:
            y_prompt_rows, y_sample_rows = _ffn((x,), norm_g, [(l, 1)], [w_ffn], [], n_prompt, True)
        conv_p_out.append(cp)
        pool_p_out.append(pp)
        z_ext = jnp.concatenate([conv_s[l], zs.reshape(steps, nb, -1)], axis=0)[-(CONV_W - 1):]
        p_ext = jnp.concatenate([pool_s[l], ps.reshape(steps, nb, -1)], axis=0)[-POOL_BUF:]
        conv_s_out.append(jnp.swapaxes(z_ext, 0, 1))
        pool_s_out.append(jnp.swapaxes(p_ext, 0, 1))
        v_s_out.append(jnp.swapaxes(vs.reshape(steps, nb, -1), 0, 1))

    y_prompt = y_prompt_rows.reshape(batch, seq, d_model)
    y_sample = jnp.swapaxes(y_sample_rows.reshape(steps, nb, d_model), 0, 1)
    return (y_prompt, y_sample, jnp.stack(conv_p_out), jnp.stack(pool_p_out),
            jnp.stack(conv_s_out), jnp.stack(pool_s_out), jnp.stack(v_s_out))
```

```python
import functools

import jax
import jax.numpy as jnp
from jax import lax
from jax.experimental import pallas as pl
from jax.experimental.pallas import tpu as pltpu

EPS = 1e-6
PAST_LEN = 16384
CHUNK = 128
N_GROUPS = 4
POOL_WINDOWS = (2, 4, 8, 16)
POOL_BUF = max(POOL_WINDOWS) - 1
CONV_W = 3

SUBLANES = 8
LANES = 128
BF16_ROWS = 16
SKEW_PERIOD = 8
HALO = 16
FFN_COL_CHUNK = 512
ROW_TILE = 512
MXU_COLS = 256
VMEM_LIMIT_BYTES = 60 * 1024 * 1024

F32 = jnp.float32
BF16 = jnp.bfloat16


def _rms(x, g):
    return x * lax.rsqrt(jnp.mean(x * x, axis=-1, keepdims=True) + EPS) * g


def _layer_norm(x, g, b):
    xc = x - jnp.mean(x, axis=-1, keepdims=True)
    return xc * lax.rsqrt(jnp.mean(xc * xc, axis=-1, keepdims=True) + EPS) * g + b


def _dot(a, b):
    return jnp.dot(a, b, preferred_element_type=F32)


def _resident(block_shape, index_map):
    return pl.BlockSpec(block_shape, index_map, pipeline_mode=pl.Buffered(1))


def _whole(a, lead=()):
    nd = a.ndim - len(lead)
    return _resident((None,) * len(lead) + a.shape[len(lead):], lambda t: tuple(lead) + (0,) * nd)


def _skewed_cols(cols):
    return cols + LANES if cols % (SKEW_PERIOD * LANES) == 0 else cols


class _Cast:
    def __init__(self, w, lead, n_prompt):
        rows, cols = w.shape[-2:]
        share = 1
        while n_prompt % share or rows % (n_prompt // share) or (rows // (n_prompt // share)) % BF16_ROWS:
            share *= 2
            assert share <= n_prompt
        n_blocks = n_prompt // share
        blk = rows // n_blocks

        def block(t):
            return jnp.minimum(t // share, n_blocks - 1)

        self.operand = w
        self.in_spec = pl.BlockSpec((None,) * len(lead) + (blk, cols), lambda t: tuple(lead) + (block(t), 0))
        self.out_spec = pl.BlockSpec((blk, _skewed_cols(cols)), lambda t: (block(t), 0))
        self.out_shape = jax.ShapeDtypeStruct((rows, _skewed_cols(cols)), BF16)


def _run_casts(cast_in, cast_out):
    for src, dst in zip(cast_in, cast_out):
        rows, cols = src.shape
        dst[:, :cols] = src[...].astype(BF16)
        if dst.shape[1] > cols:
            dst[:, cols:] = jnp.zeros((rows, dst.shape[1] - cols), BF16)


def _ffn_body(*refs, stages, two_in, two_out, n_cast, n_prompt):
    it = iter(refs)
    xp_ref = next(it)
    xs_ref = next(it) if two_in else None
    ng_ref = next(it)
    weights = [(next(it), next(it)) for _ in stages]
    cast_in = [next(it) for _ in range(n_cast)]
    op_ref = next(it)
    os_ref = next(it) if two_out else None
    cast_out = [next(it) for _ in range(n_cast)]

    t = pl.program_id(0)
    x = xp_ref[...]
    if two_in:
        x = jnp.where(t < n_prompt, x, xs_ref[...])
    for (layer, which), (wgu_ref, wdown_ref) in zip(stages, weights):
        d_ff = wdown_ref.shape[0]
        ng = ng_ref[layer]
        h = _rms(x, ng[4 * which:4 * which + 1]).astype(BF16)
        acc = None
        for c0 in range(0, d_ff, FFN_COL_CHUNK):
            c1 = min(c0 + FFN_COL_CHUNK, d_ff)
            g = _dot(h, wgu_ref[:, c0:c1])
            u = _dot(h, wgu_ref[:, d_ff + c0:d_ff + c1])
            a = (g * jax.nn.sigmoid(g) * u).astype(BF16)
            part = _dot(a, wdown_ref[c0:c1, :x.shape[1]])
            acc = part if acc is None else acc + part
        x = x + _rms(acc, 0.5 * ng[4 * which + 1:4 * which + 2])
    if two_out:
        @pl.when(t < n_prompt)
        def _():
            op_ref[...] = x

        @pl.when(t == n_prompt)
        def _():
            os_ref[...] = x
    else:
        op_ref[...] = x
    _run_casts(cast_in, cast_out)


def _ffn(xs_in, norm_g, stages, weights, casts, n_prompt, two_out):
    two_in = len(xs_in) == 2
    d = xs_in[0].shape[1]
    rows = (n_prompt + 1) * ROW_TILE
    tile = pl.BlockSpec((ROW_TILE, d), lambda t: (t, 0))
    prompt_tile = pl.BlockSpec((ROW_TILE, d), lambda t: (jnp.minimum(t, n_prompt - 1), 0))
    sample_tile = pl.BlockSpec((ROW_TILE, d), lambda t: (0, 0))
    body = functools.partial(_ffn_body, stages=tuple(stages), two_in=two_in, two_out=two_out,
                             n_cast=len(casts), n_prompt=n_prompt)
    if two_out:
        x_out_specs = [prompt_tile, sample_tile]
        x_out_shapes = [jax.ShapeDtypeStruct((n_prompt * ROW_TILE, d), F32),
                        jax.ShapeDtypeStruct((ROW_TILE, d), F32)]
    else:
        x_out_specs = [tile]
        x_out_shapes = [jax.ShapeDtypeStruct((rows, d), F32)]
    flat_weights = [w for pair in weights for w in pair]
    return pl.pallas_call(
        body,
        grid=(n_prompt + 1,),
        in_specs=([prompt_tile, sample_tile] if two_in else [tile])
        + [_whole(norm_g)] + [_whole(w) for w in flat_weights] + [c.in_spec for c in casts],
        out_specs=x_out_specs + [c.out_spec for c in casts],
        out_shape=x_out_shapes + [c.out_shape for c in casts],
        compiler_params=pltpu.CompilerParams(
            dimension_semantics=("arbitrary",), vmem_limit_bytes=VMEM_LIMIT_BYTES),
        name="ffn" + "".join(f"_l{layer}w{which}" for layer, which in stages),
    )(*xs_in, norm_g, *flat_weights, *[c.operand for c in casts])


def _project(h, win_ref, lo, hi):
    return _dot(h, win_ref[:, lo:hi])


def _join(parts):
    return jnp.concatenate(parts, axis=-1)


def _mixer_prompt_tile(j, is_last_j, x_ref, ng_ref, win_ref, ws_ref, bst_ref, lng_ref, lnb_ref, wconv_ref,
                       wpool_ref, pscale_ref, wa_ref, wb_ref, wc_ref, wo_ref,
                       o_ref, conv_out_ref, pool_out_ref, zext_ref, pext_ref, ya_ref, psum_ref,
                       *, ts, d_model, d_br):
    gw = d_br // N_GROUPS

    @pl.when(j == 0)
    def _():
        zext_ref[0:HALO, :] = jnp.zeros((HALO, d_br), F32)
        pext_ref[0:HALO, :] = jnp.zeros((HALO, d_br), F32)
        psum_ref[:, 0:SUBLANES, :] = jnp.zeros((N_GROUPS, SUBLANES, gw), F32)

    x = x_ref[...]
    ng = ng_ref[...]
    h = _rms(x, ng[2:3]).astype(BF16)

    def slabs(lo, hi):
        return [functools.partial(_project, h, win_ref, c, c + MXU_COLS) for c in range(lo, hi, MXU_COLS)]

    v_raw = _project(h, win_ref, d_br, 2 * d_br)
    u_raw = _project(h, win_ref, 0, d_br)
    u = jax.nn.gelu(u_raw)
    vb = _layer_norm(jax.nn.gelu(v_raw), lng_ref[...], lnb_ref[...]).astype(BF16)

    rows_i = lax.broadcasted_iota(jnp.int32, (CHUNK, CHUNK), 0)
    cols_i = lax.broadcasted_iota(jnp.int32, (CHUNK, CHUNK), 1)
    causal = cols_i <= rows_i
    n_chunks = ts // CHUNK

    def spatial(g):
        c0 = g * gw
        wsg = jnp.where(causal, ws_ref[g], 0.0).astype(BF16)
        mixed = _dot(wsg, _join([vb[n * CHUNK:(n + 1) * CHUNK, c0:c0 + gw] for n in range(n_chunks)]))
        bias = bst_ref[:, g:g + 1]
        for n in range(n_chunks):
            r0 = n * CHUNK
            ya_ref[r0:r0 + CHUNK, c0:c0 + gw] = (u[r0:r0 + CHUNK, c0:c0 + gw]
                                                * (mixed[:, n * gw:(n + 1) * gw] + bias))

    big = slabs(2 * d_br, 6 * d_br + 2 * d_model)
    done = []
    for k, slab in enumerate(big):
        done.append(slab())
        if k % 2 == 1 and k // 2 < N_GROUPS:
            spatial(k // 2)
    per = d_br // MXU_COLS
    xb, gb, gc, p = (_join(done[i * per:(i + 1) * per]) for i in range(4))
    per_gate = d_model // MXU_COLS
    gate_a = jax.nn.sigmoid(_join(done[4 * per:4 * per + per_gate]))
    gate_b = jax.nn.sigmoid(_join(done[4 * per + per_gate:]))
    merged = gate_a * _dot(ya_ref[...].astype(BF16), wa_ref[:, :d_model])

    z = gc * xb
    zext_ref[HALO:HALO + ts, :] = z
    wconv = wconv_ref[...]
    conv = wconv[0:1] * zext_ref[HALO - 2:HALO - 2 + ts, :]
    conv = conv + wconv[1:2] * zext_ref[HALO - 1:HALO - 1 + ts, :]
    conv = conv + wconv[2:3] * z
    yb = (gb * conv).astype(BF16)

    pext_ref[HALO:HALO + ts, :] = p
    pos = j * ts + lax.broadcasted_iota(jnp.int32, (ts, gw), 0)
    d_groups = []
    for g, w in enumerate(POOL_WINDOWS):
        c0 = g * gw
        run = pext_ref[:, c0:c0 + gw]
        shift = 1
        while shift < w:
            psum_ref[g, SUBLANES:, :] = run
            run = run + psum_ref[g, SUBLANES - shift:SUBLANES - shift + HALO + ts, :]
            shift *= 2
        cnt = jnp.minimum(pos + 1, w).astype(F32)
        d_groups.append(run[HALO:] / cnt - p[:, c0:c0 + gw])
    gate_c_parts, yc_parts = [], []
    zero_block = jnp.zeros((gw, gw), BF16)
    for k, slab in enumerate(slabs(6 * d_br + 2 * d_model, 6 * d_br + 3 * d_model)):
        gate_c_parts.append(slab())
        if k % 2 == 0:
            w_pair = jnp.concatenate([_join([wpool_ref[k], zero_block]),
                                      _join([zero_block, wpool_ref[k + 1]])], axis=0)
            yc_parts.append(_dot(_join(d_groups[k:k + 2]).astype(BF16), w_pair))
    yc = _join(yc_parts) * pscale_ref[...]

    merged = merged + gate_b * _dot(yb, wb_ref[:, :d_model])
    merged = merged + jax.nn.sigmoid(_join(gate_c_parts)) * _dot(yc.astype(BF16), wc_ref[:, :d_model])

    out = _dot(merged.astype(BF16), wo_ref[:, :d_model])
    o_ref[...] = x + _rms(out, ng[3:4])

    @pl.when(is_last_j)
    def _():
        conv_out_ref[...] = zext_ref[HALO + ts - (CONV_W - 1):HALO + ts, :]
        pool_out_ref[...] = pext_ref[HALO + ts - POOL_BUF:HALO + ts, :]

    zext_ref[0:HALO, :] = zext_ref[ts:ts + HALO, :]
    pext_ref[0:HALO, :] = pext_ref[ts:ts + HALO, :]


def _mixer_sample_tile(ws_ref, bs_ref, x_ref, conv_ref, pool_ref, ng_ref, win_ref, lng_ref, lnb_ref,
                       wconv_ref, wpool_ref, pscale_ref, wa_ref, wb_ref, wc_ref, wo_ref,
                       o_ref, v_out_ref, z_out_ref, p_out_ref,
                       *, layer, nb, steps, d_model, d_br, start_pos):
    gw = d_br // N_GROUPS
    x = x_ref[...]
    ng = ng_ref[...]
    h = _rms(x, ng[2:3]).astype(BF16)

    def blk(a, i):
        return a[i * nb:(i + 1) * nb]

    u = jax.nn.gelu(_project(h, win_ref, 0, d_br))
    v = _layer_norm(jax.nn.gelu(_project(h, win_ref, d_br, 2 * d_br)), lng_ref[...], lnb_ref[...])
    v_out_ref[...] = v
    ya_rows = []
    for t in range(steps):
        cols = []
        for g in range(N_GROUPS):
            c0 = g * gw
            mixed = None
            for s in range(t + 1):
                term = ws_ref[layer, g * steps * steps + t * steps + s] * blk(v, s)[:, c0:c0 + gw]
                mixed = term if mixed is None else mixed + term
            cols.append(mixed + bs_ref[layer, g * steps + t])
        ya_rows.append(blk(u, t) * _join(cols))
    ya = jnp.concatenate(ya_rows, axis=0)

    z = _project(h, win_ref, 4 * d_br, 5 * d_br) * _project(h, win_ref, 2 * d_br, 3 * d_br)
    z_out_ref[...] = z
    z_ext = [conv_ref[k] for k in range(CONV_W - 1)] + [blk(z, i) for i in range(steps)]
    wconv = wconv_ref[...]
    conv_rows = []
    for i in range(steps):
        c = wconv[0:1] * z_ext[i]
        for k in range(1, CONV_W):
            c = c + wconv[k:k + 1] * z_ext[i + k]
        conv_rows.append(c)
    yb = _project(h, win_ref, 3 * d_br, 4 * d_br) * jnp.concatenate(conv_rows, axis=0)

    p = _project(h, win_ref, 5 * d_br, 6 * d_br)
    p_out_ref[...] = p
    p_ext = [pool_ref[k] for k in range(POOL_BUF)] + [blk(p, i) for i in range(steps)]
    d_groups = []
    for g, w in enumerate(POOL_WINDOWS):
        c0 = g * gw
        rows_g = []
        for i in range(steps):
            s = p_ext[POOL_BUF + i][:, c0:c0 + gw]
            for k in range(1, w):
                s = s + p_ext[POOL_BUF + i - k][:, c0:c0 + gw]
            cnt = float(min(start_pos + i + 1, w))
            rows_g.append(s / cnt - p_ext[POOL_BUF + i][:, c0:c0 + gw])
        d_groups.append(jnp.concatenate(rows_g, axis=0))
    yc = _join([_dot(d_groups[g].astype(BF16), wpool_ref[g]) for g in range(N_GROUPS)]) * pscale_ref[...]

    merged = None
    for k, (y, w_ref) in enumerate(((ya, wa_ref), (yb, wb_ref), (yc, wc_ref))):
        lo = 6 * d_br + k * d_model
        term = jax.nn.sigmoid(_project(h, win_ref, lo, lo + d_model)) * _dot(y.astype(BF16), w_ref[:, :d_model])
        merged = term if merged is None else merged + term
    out = _dot(merged.astype(BF16), wo_ref[:, :d_model])
    o_ref[...] = x + _rms(out, ng[3:4])


def _mixer_body(*refs, layer, n_prompt, nj, n_cast, ts, nb, steps, d_model, d_br):
    it = iter(refs)
    ws_sm, bs_sm, x_ref, conv_s_ref, pool_s_ref = (next(it) for _ in range(5))
    (ng_ref, win_ref, ws_ref, bst_ref, lng_ref, lnb_ref, wconv_ref, wpool_ref, pscale_ref,
     wa_ref, wb_ref, wc_ref, wo_ref) = (next(it) for _ in range(13))
    cast_in = [next(it) for _ in range(n_cast)]
    o_ref, conv_out_ref, pool_out_ref, v_out_ref, z_out_ref, p_out_ref = (next(it) for _ in range(6))
    cast_out = [next(it) for _ in range(n_cast)]
    zext_ref, pext_ref, ya_ref, psum_ref = next(it), next(it), next(it), next(it)

    t = pl.program_id(0)

    @pl.when(t < n_prompt)
    def _():
        j = lax.rem(t, nj)
        _mixer_prompt_tile(j, j == nj - 1, x_ref, ng_ref, win_ref, ws_ref, bst_ref, lng_ref, lnb_ref,
                           wconv_ref, wpool_ref, pscale_ref, wa_ref, wb_ref, wc_ref, wo_ref,
                           o_ref, conv_out_ref, pool_out_ref, zext_ref, pext_ref, ya_ref, psum_ref,
                           ts=ts, d_model=d_model, d_br=d_br)

    @pl.when(t == n_prompt)
    def _():
        _mixer_sample_tile(ws_sm, bs_sm, x_ref, conv_s_ref, pool_s_ref, ng_ref, win_ref, lng_ref, lnb_ref,
                           wconv_ref, wpool_ref, pscale_ref, wa_ref, wb_ref, wc_ref, wo_ref,
                           o_ref, v_out_ref, z_out_ref, p_out_ref,
                           layer=layer, nb=nb, steps=steps, d_model=d_model, d_br=d_br, start_pos=PAST_LEN)

    _run_casts(cast_in, cast_out)


def _mixer(x, batch, steps, conv_s, pool_s, ws_small, bs_small, norm_g, win, w_s, b_s_t, ln_g, ln_b,
           w_conv, wpool, pool_scale, wa, wb, wc, wo, casts, layer, n_prompt):
    rows, d_model = x.shape
    d_br = wa.shape[0]
    ts = ROW_TILE
    nj = n_prompt // batch
    nb = ROW_TILE // steps
    assert n_prompt % batch == 0 and ts % CHUNK == 0 and ts >= HALO
    assert steps <= CHUNK and nb * steps == ROW_TILE and nb % SUBLANES == 0
    assert d_br % MXU_COLS == 0 and d_model // MXU_COLS == N_GROUPS
    body = functools.partial(_mixer_body, layer=layer, n_prompt=n_prompt, nj=nj, n_cast=len(casts),
                             ts=ts, nb=nb, steps=steps, d_model=d_model, d_br=d_br)
    smem = pl.BlockSpec(memory_space=pltpu.SMEM)
    tile = pl.BlockSpec((ts, d_model), lambda t: (t, 0))
    seq_of = lambda t: jnp.minimum(t // nj, batch - 1)
    sample_rows = pl.BlockSpec((ROW_TILE, d_br), lambda t: (0, 0))
    layer_ops = (w_s, b_s_t, ln_g, ln_b, w_conv, wpool, pool_scale)
    out_proj = (wa, wb, wc, wo)
    return pl.pallas_call(
        body,
        grid=(n_prompt + 1,),
        in_specs=[smem, smem, tile, _whole(conv_s, (layer,)), _whole(pool_s, (layer,)),
                  _whole(norm_g, (layer,)), _whole(win)]
        + [_whole(a, (layer,)) for a in layer_ops] + [_whole(a) for a in out_proj]
        + [c.in_spec for c in casts],
        out_specs=[
            tile,
            pl.BlockSpec((None, CONV_W - 1, d_br), lambda t: (seq_of(t), 0, 0)),
            pl.BlockSpec((None, POOL_BUF, d_br), lambda t: (seq_of(t), 0, 0)),
            sample_rows, sample_rows, sample_rows,
        ] + [c.out_spec for c in casts],
        out_shape=[
            jax.ShapeDtypeStruct((rows, d_model), F32),
            jax.ShapeDtypeStruct((batch, CONV_W - 1, d_br), F32),
            jax.ShapeDtypeStruct((batch, POOL_BUF, d_br), F32),
            jax.ShapeDtypeStruct((ROW_TILE, d_br), F32),
            jax.ShapeDtypeStruct((ROW_TILE, d_br), F32),
            jax.ShapeDtypeStruct((ROW_TILE, d_br), F32),
        ] + [c.out_shape for c in casts],
        scratch_shapes=[
            pltpu.VMEM((HALO + ts, d_br), F32),
            pltpu.VMEM((HALO + ts, d_br), F32),
            pltpu.VMEM((ts, d_br), F32),
            pltpu.VMEM((N_GROUPS, SUBLANES + HALO + ts, d_br // N_GROUPS), F32),
        ],
        compiler_params=pltpu.CompilerParams(
            dimension_semantics=("arbitrary",), vmem_limit_bytes=VMEM_LIMIT_BYTES),
        name=f"mixer_l{layer}",
    )(ws_small, bs_small, x, conv_s, pool_s, norm_g, win, *layer_ops, *out_proj,
      *[c.operand for c in casts])


def kernel(x_prompt, x_sample, state_conv, state_pool, norm_g, w_ffn_gu, w_ffn_down, w_in, w_s, b_s,
           ln_g, ln_b, w_conv, w_pool, pool_scale, w_a_out, w_b_out, w_c_out, w_o):
    depth = w_in.shape[0]
    batch, seq, d_model = x_prompt.shape
    nb, steps, _ = x_sample.shape
    assert (batch * seq) % ROW_TILE == 0 and nb * steps == ROW_TILE
    n_prompt = batch * seq // ROW_TILE

    wpool = w_pool.astype(BF16)
    lng, lnb, pscale = (a[:, None, :] for a in (ln_g, ln_b, pool_scale))
    b_s_t = jnp.swapaxes(b_s, 1, 2)
    ws_small = w_s[:, :, :steps, :steps].reshape(depth, -1)
    bs_small = b_s[:, :, :steps].reshape(depth, -1)
    conv_s = jnp.swapaxes(state_conv, 1, 2)
    pool_s = jnp.swapaxes(state_pool, 1, 2)

    x_in = (x_prompt.reshape(batch * seq, d_model),
            jnp.swapaxes(x_sample, 0, 1).reshape(steps * nb, d_model))
    w_ffn = (w_ffn_gu[0, 0].astype(BF16), w_ffn_down[0, 0].astype(BF16))

    def ffn_casts(layer, which):
        return [_Cast(w_ffn_gu, (layer, which), n_prompt), _Cast(w_ffn_down, (layer, which), n_prompt)]

    conv_p_out, pool_p_out, conv_s_out, pool_s_out, v_s_out = [], [], [], [], []
    for l in range(depth):
        mixer_casts = [_Cast(w, (l,), n_prompt) for w in (w_in, w_a_out, w_b_out, w_c_out, w_o)]
        x, win, wa, wb, wc, wo = _ffn(x_in, norm_g, [(l, 0)], [w_ffn], mixer_casts, n_prompt, False)
        x, cp, pp, vs, zs, ps, *w_ffn = _mixer(
            x, batch, steps, conv_s, pool_s, ws_small, bs_small, norm_g, win, w_s, b_s_t, lng, lnb,
            w_conv, wpool, pscale, wa, wb, wc, wo, ffn_casts(l, 1), l, n_prompt)
        if l + 1 < depth:
            x, *w_ffn = _ffn((x,), norm_g, [(l, 1)], [w_ffn], ffn_casts(l + 1, 0), n_prompt, False)
            x_in = (x,)
        else:
            y_prompt_rows, y_sample_rows = _ffn((x,), norm_g, [(l, 1)], [w_ffn], [], n_prompt, True)
        conv_p_out.append(cp)
        pool_p_out.append(pp)
        z_ext = jnp.concatenate([conv_s[l], zs.reshape(steps, nb, -1)], axis=0)[-(CONV_W - 1):]
        p_ext = jnp.concatenate([pool_s[l], ps.reshape(steps, nb, -1)], axis=0)[-POOL_BUF:]
        conv_s_out.append(jnp.swapaxes(z_ext, 0, 1))
        pool_s_out.append(jnp.swapaxes(p_ext, 0, 1))
        v_s_out.append(jnp.swapaxes(vs.reshape(steps, nb, -1), 0, 1))

    y_prompt = y_prompt_rows.reshape(batch, seq, d_model)
    y_sample = jnp.swapaxes(y_sample_rows.reshape(steps, nb, d_model), 0, 1)
    return (y_prompt, y_sample, jnp.stack(conv_p_out), jnp.stack(pool_p_out),
            jnp.stack(conv_s_out), jnp.stack(pool_s_out), jnp.stack(v_s_out))
```

```python
import functools

import jax
import jax.numpy as jnp
from jax import lax
from jax.experimental import pallas as pl
from jax.experimental.pallas import tpu as pltpu

EPS = 1e-6
PAST_LEN = 16384
CHUNK = 128
N_GROUPS = 4
POOL_WINDOWS = (2, 4, 8, 16)
POOL_BUF = max(POOL_WINDOWS) - 1
CONV_W = 3

SUBLANES = 8
LANES = 128
BF16_ROWS = 16
SKEW_PERIOD = 8
HALO = 16
FFN_COL_CHUNK = 512
ROW_TILE = 512
MXU_COLS = 256
VMEM_LIMIT_BYTES = 60 * 1024 * 1024

F32 = jnp.float32
BF16 = jnp.bfloat16


def _rms(x, g):
    return x * lax.rsqrt(jnp.mean(x * x, axis=-1, keepdims=True) + EPS) * g


def _layer_norm(x, g, b):
    xc = x - jnp.mean(x, axis=-1, keepdims=True)
    return xc * lax.rsqrt(jnp.mean(xc * xc, axis=-1, keepdims=True) + EPS) * g + b


def _dot(a, b):
    return jnp.dot(a, b, preferred_element_type=F32)


def _resident(block_shape, index_map):
    return pl.BlockSpec(block_shape, index_map, pipeline_mode=pl.Buffered(1))


def _whole(a, lead=()):
    nd = a.ndim - len(lead)
    return _resident((None,) * len(lead) + a.shape[len(lead):], lambda t: tuple(lead) + (0,) * nd)


def _skewed_cols(cols):
    return cols + LANES if cols % (SKEW_PERIOD * LANES) == 0 else cols


class _Cast:
    def __init__(self, w, lead, n_prompt):
        rows, cols = w.shape[-2:]
        share = 1
        while n_prompt % share or rows % (n_prompt // share) or (rows // (n_prompt // share)) % BF16_ROWS:
            share *= 2
            assert share <= n_prompt
        n_blocks = n_prompt // share
        blk = rows // n_blocks

        def block(t):
            return jnp.minimum(t // share, n_blocks - 1)

        self.operand = w
        self.in_spec = pl.BlockSpec((None,) * len(lead) + (blk, cols), lambda t: tuple(lead) + (block(t), 0))
        self.out_spec = pl.BlockSpec((blk, _skewed_cols(cols)), lambda t: (block(t), 0))
        self.out_shape = jax.ShapeDtypeStruct((rows, _skewed_cols(cols)), BF16)


def _run_casts(cast_in, cast_out):
    for src, dst in zip(cast_in, cast_out):
        rows, cols = src.shape
        dst[:, :cols] = src[...].astype(BF16)
        if dst.shape[1] > cols:
            dst[:, cols:] = jnp.zeros((rows, dst.shape[1] - cols), BF16)


def _ffn_body(*refs, stages, two_in, two_out, n_cast, n_prompt):
    it = iter(refs)
    xp_ref = next(it)
    xs_ref = next(it) if two_in else None
    ng_ref = next(it)
    weights = [(next(it), next(it)) for _ in stages]
    cast_in = [next(it) for _ in range(n_cast)]
    op_ref = next(it)
    os_ref = next(it) if two_out else None
    cast_out = [next(it) for _ in range(n_cast)]

    t = pl.program_id(0)
    x = xp_ref[...]
    if two_in:
        x = jnp.where(t < n_prompt, x, xs_ref[...])
    for (layer, which), (wgu_ref, wdown_ref) in zip(stages, weights):
        d_ff = wdown_ref.shape[0]
        ng = ng_ref[layer]
        xg = (x * ng[4 * which:4 * which + 1]).astype(BF16)
        r = lax.rsqrt(jnp.mean(x * x, axis=-1, keepdims=True) + EPS)
        acc = None
        for c0 in range(0, d_ff, FFN_COL_CHUNK):
            c1 = min(c0 + FFN_COL_CHUNK, d_ff)
            g = _dot(xg, wgu_ref[:, c0:c1]) * r
            u = _dot(xg, wgu_ref[:, d_ff + c0:d_ff + c1]) * r
            a = (g * jax.nn.sigmoid(g) * u).astype(BF16)
            part = _dot(a, wdown_ref[c0:c1, :x.shape[1]])
            acc = part if acc is None else acc + part
        x = x + _rms(acc, 0.5 * ng[4 * which + 1:4 * which + 2])
    if two_out:
        @pl.when(t < n_prompt)
        def _():
            op_ref[...] = x

        @pl.when(t == n_prompt)
        def _():
            os_ref[...] = x
    else:
        op_ref[...] = x
    _run_casts(cast_in, cast_out)


def _ffn(xs_in, norm_g, stages, weights, casts, n_prompt, two_out):
    two_in = len(xs_in) == 2
    d = xs_in[0].shape[1]
    rows = (n_prompt + 1) * ROW_TILE
    tile = pl.BlockSpec((ROW_TILE, d), lambda t: (t, 0))
    prompt_tile = pl.BlockSpec((ROW_TILE, d), lambda t: (jnp.minimum(t, n_prompt - 1), 0))
    sample_tile = pl.BlockSpec((ROW_TILE, d), lambda t: (0, 0))
    body = functools.partial(_ffn_body, stages=tuple(stages), two_in=two_in, two_out=two_out,
                             n_cast=len(casts), n_prompt=n_prompt)
    if two_out:
        x_out_specs = [prompt_tile, sample_tile]
        x_out_shapes = [jax.ShapeDtypeStruct((n_prompt * ROW_TILE, d), F32),
                        jax.ShapeDtypeStruct((ROW_TILE, d), F32)]
    else:
        x_out_specs = [tile]
        x_out_shapes = [jax.ShapeDtypeStruct((rows, d), F32)]
    flat_weights = [w for pair in weights for w in pair]
    return pl.pallas_call(
        body,
        grid=(n_prompt + 1,),
        in_specs=([prompt_tile, sample_tile] if two_in else [tile])
        + [_whole(norm_g)] + [_whole(w) for w in flat_weights] + [c.in_spec for c in casts],
        out_specs=x_out_specs + [c.out_spec for c in casts],
        out_shape=x_out_shapes + [c.out_shape for c in casts],
        compiler_params=pltpu.CompilerParams(
            dimension_semantics=("arbitrary",), vmem_limit_bytes=VMEM_LIMIT_BYTES),
        name="ffn" + "".join(f"_l{layer}w{which}" for layer, which in stages),
    )(*xs_in, norm_g, *flat_weights, *[c.operand for c in casts])


def _project(h, win_ref, lo, hi):
    return _dot(h, win_ref[:, lo:hi])


def _join(parts):
    return jnp.concatenate(parts, axis=-1)


def _mixer_prompt_tile(j, is_last_j, x_ref, ng_ref, win_ref, ws_ref, bst_ref, lng_ref, lnb_ref, wconv_ref,
                       wpool_ref, pscale_ref, wa_ref, wb_ref, wc_ref, wo_ref,
                       o_ref, conv_out_ref, pool_out_ref, zext_ref, pext_ref, ya_ref, psum_ref,
                       *, ts, d_model, d_br):
    gw = d_br // N_GROUPS

    @pl.when(j == 0)
    def _():
        zext_ref[0:HALO, :] = jnp.zeros((HALO, d_br), F32)
        pext_ref[0:HALO, :] = jnp.zeros((HALO, d_br), F32)
        psum_ref[:, 0:SUBLANES, :] = jnp.zeros((N_GROUPS, SUBLANES, gw), F32)

    x = x_ref[...]
    ng = ng_ref[...]
    h = _rms(x, ng[2:3]).astype(BF16)

    def slabs(lo, hi):
        return [functools.partial(_project, h, win_ref, c, c + MXU_COLS) for c in range(lo, hi, MXU_COLS)]

    v_raw = _project(h, win_ref, d_br, 2 * d_br)
    u_raw = _project(h, win_ref, 0, d_br)
    u = jax.nn.gelu(u_raw)
    vb = _layer_norm(jax.nn.gelu(v_raw), lng_ref[...], lnb_ref[...]).astype(BF16)

    rows_i = lax.broadcasted_iota(jnp.int32, (CHUNK, CHUNK), 0)
    cols_i = lax.broadcasted_iota(jnp.int32, (CHUNK, CHUNK), 1)
    causal = cols_i <= rows_i
    n_chunks = ts // CHUNK

    def spatial(g):
        c0 = g * gw
        wsg = jnp.where(causal, ws_ref[g], 0.0).astype(BF16)
        mixed = _dot(wsg, _join([vb[n * CHUNK:(n + 1) * CHUNK, c0:c0 + gw] for n in range(n_chunks)]))
        bias = bst_ref[:, g:g + 1]
        for n in range(n_chunks):
            r0 = n * CHUNK
            ya_ref[r0:r0 + CHUNK, c0:c0 + gw] = (u[r0:r0 + CHUNK, c0:c0 + gw]
                                                * (mixed[:, n * gw:(n + 1) * gw] + bias))

    big = slabs(2 * d_br, 6 * d_br + 2 * d_model)
    done = []
    for k, slab in enumerate(big):
        done.append(slab())
        if k % 2 == 1 and k // 2 < N_GROUPS:
            spatial(k // 2)
    per = d_br // MXU_COLS
    xb, gb, gc, p = (_join(done[i * per:(i + 1) * per]) for i in range(4))
    per_gate = d_model // MXU_COLS
    gate_a = jax.nn.sigmoid(_join(done[4 * per:4 * per + per_gate]))
    gate_b = jax.nn.sigmoid(_join(done[4 * per + per_gate:]))
    merged = gate_a * _dot(ya_ref[...].astype(BF16), wa_ref[:, :d_model])

    z = gc * xb
    zext_ref[HALO:HALO + ts, :] = z
    wconv = wconv_ref[...]
    conv = wconv[0:1] * zext_ref[HALO - 2:HALO - 2 + ts, :]
    conv = conv + wconv[1:2] * zext_ref[HALO - 1:HALO - 1 + ts, :]
    conv = conv + wconv[2:3] * z
    yb = (gb * conv).astype(BF16)

    pext_ref[HALO:HALO + ts, :] = p
    pos = j * ts + lax.broadcasted_iota(jnp.int32, (ts, gw), 0)
    d_groups = []
    for g, w in enumerate(POOL_WINDOWS):
        c0 = g * gw
        run = pext_ref[:, c0:c0 + gw]
        shift = 1
        while shift < w:
            psum_ref[g, SUBLANES:, :] = run
            run = run + psum_ref[g, SUBLANES - shift:SUBLANES - shift + HALO + ts, :]
            shift *= 2
        cnt = jnp.minimum(pos + 1, w).astype(F32)
        d_groups.append(run[HALO:] / cnt - p[:, c0:c0 + gw])
    gate_c_parts, yc_parts = [], []
    zero_block = jnp.zeros((gw, gw), BF16)
    for k, slab in enumerate(slabs(6 * d_br + 2 * d_model, 6 * d_br + 3 * d_model)):
        gate_c_parts.append(slab())
        if k % 2 == 0:
            w_pair = jnp.concatenate([_join([wpool_ref[k], zero_block]),
                                      _join([zero_block, wpool_ref[k + 1]])], axis=0)
            yc_parts.append(_dot(_join(d_groups[k:k + 2]).astype(BF16), w_pair))
    yc = _join(yc_parts) * pscale_ref[...]

    merged = merged + gate_b * _dot(yb, wb_ref[:, :d_model])
    merged = merged + jax.nn.sigmoid(_join(gate_c_parts)) * _dot(yc.astype(BF16), wc_ref[:, :d_model])

    out = _dot(merged.astype(BF16), wo_ref[:, :d_model])
    o_ref[...] = x + _rms(out, ng[3:4])

    @pl.when(is_last_j)
    def _():
        conv_out_ref[...] = zext_ref[HALO + ts - (CONV_W - 1):HALO + ts, :]
        pool_out_ref[...] = pext_ref[HALO + ts - POOL_BUF:HALO + ts, :]

    zext_ref[0:HALO, :] = zext_ref[ts:ts + HALO, :]
    pext_ref[0:HALO, :] = pext_ref[ts:ts + HALO, :]


def _mixer_sample_tile(ws_ref, bs_ref, x_ref, conv_ref, pool_ref, ng_ref, win_ref, lng_ref, lnb_ref,
                       wconv_ref, wpool_ref, pscale_ref, wa_ref, wb_ref, wc_ref, wo_ref,
                       o_ref, v_out_ref, z_out_ref, p_out_ref,
                       *, layer, nb, steps, d_model, d_br, start_pos):
    gw = d_br // N_GROUPS
    x = x_ref[...]
    ng = ng_ref[...]
    h = _rms(x, ng[2:3]).astype(BF16)

    def blk(a, i):
        return a[i * nb:(i + 1) * nb]

    u = jax.nn.gelu(_project(h, win_ref, 0, d_br))
    v = _layer_norm(jax.nn.gelu(_project(h, win_ref, d_br, 2 * d_br)), lng_ref[...], lnb_ref[...])
    v_out_ref[...] = v
    ya_rows = []
    for t in range(steps):
        cols = []
        for g in range(N_GROUPS):
            c0 = g * gw
            mixed = None
            for s in range(t + 1):
                term = ws_ref[layer, g * steps * steps + t * steps + s] * blk(v, s)[:, c0:c0 + gw]
                mixed = term if mixed is None else mixed + term
            cols.append(mixed + bs_ref[layer, g * steps + t])
        ya_rows.append(blk(u, t) * _join(cols))
    ya = jnp.concatenate(ya_rows, axis=0)

    z = _project(h, win_ref, 4 * d_br, 5 * d_br) * _project(h, win_ref, 2 * d_br, 3 * d_br)
    z_out_ref[...] = z
    z_ext = [conv_ref[k] for k in range(CONV_W - 1)] + [blk(z, i) for i in range(steps)]
    wconv = wconv_ref[...]
    conv_rows = []
    for i in range(steps):
        c = wconv[0:1] * z_ext[i]
        for k in range(1, CONV_W):
            c = c + wconv[k:k + 1] * z_ext[i + k]
        conv_rows.append(c)
    yb = _project(h, win_ref, 3 * d_br, 4 * d_br) * jnp.concatenate(conv_rows, axis=0)

    p = _project(h, win_ref, 5 * d_br, 6 * d_br)
    p_out_ref[...] = p
    p_ext = [pool_ref[k] for k in range(POOL_BUF)] + [blk(p, i) for i in range(steps)]
    d_groups = []
    for g, w in enumerate(POOL_WINDOWS):
        c0 = g * gw
        rows_g = []
        for i in range(steps):
            s = p_ext[POOL_BUF + i][:, c0:c0 + gw]
            for k in range(1, w):
                s = s + p_ext[POOL_BUF + i - k][:, c0:c0 + gw]
            cnt = float(min(start_pos + i + 1, w))
            rows_g.append(s / cnt - p_ext[POOL_BUF + i][:, c0:c0 + gw])
        d_groups.append(jnp.concatenate(rows_g, axis=0))
    yc = _join([_dot(d_groups[g].astype(BF16), wpool_ref[g]) for g in range(N_GROUPS)]) * pscale_ref[...]

    merged = None
    for k, (y, w_ref) in enumerate(((ya, wa_ref), (yb, wb_ref), (yc, wc_ref))):
        lo = 6 * d_br + k * d_model
        term = jax.nn.sigmoid(_project(h, win_ref, lo, lo + d_model)) * _dot(y.astype(BF16), w_ref[:, :d_model])
        merged = term if merged is None else merged + term
    out = _dot(merged.astype(BF16), wo_ref[:, :d_model])
    o_ref[...] = x + _rms(out, ng[3:4])


def _mixer_body(*refs, layer, n_prompt, nj, n_cast, ts, nb, steps, d_model, d_br):
    it = iter(refs)
    ws_sm, bs_sm, x_ref, conv_s_ref, pool_s_ref = (next(it) for _ in range(5))
    (ng_ref, win_ref, ws_ref, bst_ref, lng_ref, lnb_ref, wconv_ref, wpool_ref, pscale_ref,
     wa_ref, wb_ref, wc_ref, wo_ref) = (next(it) for _ in range(13))
    cast_in = [next(it) for _ in range(n_cast)]
    o_ref, conv_out_ref, pool_out_ref, v_out_ref, z_out_ref, p_out_ref = (next(it) for _ in range(6))
    cast_out = [next(it) for _ in range(n_cast)]
    zext_ref, pext_ref, ya_ref, psum_ref = next(it), next(it), next(it), next(it)

    t = pl.program_id(0)

    @pl.when(t < n_prompt)
    def _():
        j = lax.rem(t, nj)
        _mixer_prompt_tile(j, j == nj - 1, x_ref, ng_ref, win_ref, ws_ref, bst_ref, lng_ref, lnb_ref,
                           wconv_ref, wpool_ref, pscale_ref, wa_ref, wb_ref, wc_ref, wo_ref,
                           o_ref, conv_out_ref, pool_out_ref, zext_ref, pext_ref, ya_ref, psum_ref,
                           ts=ts, d_model=d_model, d_br=d_br)

    @pl.when(t == n_prompt)
    def _():
        _mixer_sample_tile(ws_sm, bs_sm, x_ref, conv_s_ref, pool_s_ref, ng_ref, win_ref, lng_ref, lnb_ref,
                           wconv_ref, wpool_ref, pscale_ref, wa_ref, wb_ref, wc_ref, wo_ref,
                           o_ref, v_out_ref, z_out_ref, p_out_ref,
                           layer=layer, nb=nb, steps=steps, d_model=d_model, d_br=d_br, start_pos=PAST_LEN)

    _run_casts(cast_in, cast_out)


def _mixer(x, batch, steps, conv_s, pool_s, ws_small, bs_small, norm_g, win, w_s, b_s_t, ln_g, ln_b,
           w_conv, wpool, pool_scale, wa, wb, wc, wo, casts, layer, n_prompt):
    rows, d_model = x.shape
    d_br = wa.shape[0]
    ts = ROW_TILE
    nj = n_prompt // batch
    nb = ROW_TILE // steps
    assert n_prompt % batch == 0 and ts % CHUNK == 0 and ts >= HALO
    assert steps <= CHUNK and nb * steps == ROW_TILE and nb % SUBLANES == 0
    assert d_br % MXU_COLS == 0 and d_model // MXU_COLS == N_GROUPS
    body = functools.partial(_mixer_body, layer=layer, n_prompt=n_prompt, nj=nj, n_cast=len(casts),
                             ts=ts, nb=nb, steps=steps, d_model=d_model, d_br=d_br)
    smem = pl.BlockSpec(memory_space=pltpu.SMEM)
    tile = pl.BlockSpec((ts, d_model), lambda t: (t, 0))
    seq_of = lambda t: jnp.minimum(t // nj, batch - 1)
    sample_rows = pl.BlockSpec((ROW_TILE, d_br), lambda t: (0, 0))
    layer_ops = (w_s, b_s_t, ln_g, ln_b, w_conv, wpool, pool_scale)
    out_proj = (wa, wb, wc, wo)
    return pl.pallas_call(
        body,
        grid=(n_prompt + 1,),
        in_specs=[smem, smem, tile, _whole(conv_s, (layer,)), _whole(pool_s, (layer,)),
                  _whole(norm_g, (layer,)), _whole(win)]
        + [_whole(a, (layer,)) for a in layer_ops] + [_whole(a) for a in out_proj]
        + [c.in_spec for c in casts],
        out_specs=[
            tile,
            pl.BlockSpec((None, CONV_W - 1, d_br), lambda t: (seq_of(t), 0, 0)),
            pl.BlockSpec((None, POOL_BUF, d_br), lambda t: (seq_of(t), 0, 0)),
            sample_rows, sample_rows, sample_rows,
        ] + [c.out_spec for c in casts],
        out_shape=[
            jax.ShapeDtypeStruct((rows, d_model), F32),
            jax.ShapeDtypeStruct((batch, CONV_W - 1, d_br), F32),
            jax.ShapeDtypeStruct((batch, POOL_BUF, d_br), F32),
            jax.ShapeDtypeStruct((ROW_TILE, d_br), F32),
            jax.ShapeDtypeStruct((ROW_TILE, d_br), F32),
            jax.ShapeDtypeStruct((ROW_TILE, d_br), F32),
        ] + [c.out_shape for c in casts],
        scratch_shapes=[
            pltpu.VMEM((HALO + ts, d_br), F32),
            pltpu.VMEM((HALO + ts, d_br), F32),
            pltpu.VMEM((ts, d_br), F32),
            pltpu.VMEM((N_GROUPS, SUBLANES + HALO + ts, d_br // N_GROUPS), F32),
        ],
        compiler_params=pltpu.CompilerParams(
            dimension_semantics=("arbitrary",), vmem_limit_bytes=VMEM_LIMIT_BYTES),
        name=f"mixer_l{layer}",
    )(ws_small, bs_small, x, conv_s, pool_s, norm_g, win, *layer_ops, *out_proj,
      *[c.operand for c in casts])


def kernel(x_prompt, x_sample, state_conv, state_pool, norm_g, w_ffn_gu, w_ffn_down, w_in, w_s, b_s,
           ln_g, ln_b, w_conv, w_pool, pool_scale, w_a_out, w_b_out, w_c_out, w_o):
    depth = w_in.shape[0]
    batch, seq, d_model = x_prompt.shape
    nb, steps, _ = x_sample.shape
    assert (batch * seq) % ROW_TILE == 0 and nb * steps == ROW_TILE
    n_prompt = batch * seq // ROW_TILE

    wpool = w_pool.astype(BF16)
    lng, lnb, pscale = (a[:, None, :] for a in (ln_g, ln_b, pool_scale))
    b_s_t = jnp.swapaxes(b_s, 1, 2)
    ws_small = w_s[:, :, :steps, :steps].reshape(depth, -1)
    bs_small = b_s[:, :, :steps].reshape(depth, -1)
    conv_s = jnp.swapaxes(state_conv, 1, 2)
    pool_s = jnp.swapaxes(state_pool, 1, 2)

    x_in = (x_prompt.reshape(batch * seq, d_model),
            jnp.swapaxes(x_sample, 0, 1).reshape(steps * nb, d_model))
    w_ffn = (w_ffn_gu[0, 0].astype(BF16), w_ffn_down[0, 0].astype(BF16))

    def ffn_casts(layer, which):
        return [_Cast(w_ffn_gu, (layer, which), n_prompt), _Cast(w_ffn_down, (layer, which), n_prompt)]

    conv_p_out, pool_p_out, conv_s_out, pool_s_out, v_s_out = [], [], [], [], []
    for l in range(depth):
        mixer_casts = [_Cast(w, (l,), n_prompt) for w in (w_in, w_a_out, w_b_out, w_c_out, w_o)]
        x, win, wa, wb, wc, wo = _ffn(x_in, norm_g, [(l, 0)], [w_ffn], mixer_casts, n_prompt, False)
        x, cp, pp, vs, zs, ps, *w_ffn = _mixer(
            x, batch, steps, conv_s, pool_s, ws_small, bs_small, norm_g, win, w_s, b_s_t, lng, lnb,
            w_conv, wpool, pscale, wa, wb, wc, wo, ffn_casts(l, 1), l, n_prompt)
        if l + 1 < depth:
            x, *w_ffn = _ffn((x,), norm_g, [(l, 1)], [w_ffn], ffn_casts(l + 1, 0), n_prompt, False)
            x_in = (x,)
        else:
            y_prompt_rows, y_sample_rows = _ffn((x,), norm_g, [(l, 1)], [w_ffn], [], n_prompt, True)
        conv_p_out.append(cp)
        pool_p_out.append(pp)
        z_ext = jnp.concatenate([conv_s[l], zs.reshape(steps, nb, -1)], axis=0)[-(CONV_W - 1):]
        p_ext = jnp.concatenate([pool_s[l], ps.reshape(steps, nb, -1)], axis=0)[-POOL_BUF:]
        conv_s_out.append(jnp.swapaxes(z_ext, 0, 1))
        pool_s_out.append(jnp.swapaxes(p_ext, 0, 1))
        v_s_out.append(jnp.swapaxes(vs.reshape(steps, nb, -1), 0, 1))

    y_prompt = y_prompt_rows.reshape(batch, seq, d_model)
    y_sample = jnp.swapaxes(y_sample_rows.reshape(steps, nb, d_model), 0, 1)
    return (y_prompt, y_sample, jnp.stack(conv_p_out), jnp.stack(pool_p_out),
            jnp.stack(conv_s_out), jnp.stack(pool_s_out), jnp.stack(v_s_out))
```

```python
import functools

import jax
import jax.numpy as jnp
from jax import lax
from jax.experimental import pallas as pl
from jax.experimental.pallas import tpu as pltpu

EPS = 1e-6
PAST_LEN = 16384
CHUNK = 128
N_GROUPS = 4
POOL_WINDOWS = (2, 4, 8, 16)
POOL_BUF = max(POOL_WINDOWS) - 1
CONV_W = 3

SUBLANES = 8
LANES = 128
BF16_ROWS = 16
SKEW_PERIOD = 8
HALO = 16
FFN_COL_CHUNK = 512
ROW_TILE = 512
MXU_COLS = 256
VMEM_LIMIT_BYTES = 60 * 1024 * 1024

F32 = jnp.float32
BF16 = jnp.bfloat16


def _rms(x, g):
    return x * lax.rsqrt(jnp.mean(x * x, axis=-1, keepdims=True) + EPS) * g


def _split_norm(x, g):
    return (x * g).astype(BF16), lax.rsqrt(jnp.mean(x * x, axis=-1, keepdims=True) + EPS)


def _layer_norm(x, g, b):
    xc = x - jnp.mean(x, axis=-1, keepdims=True)
    return xc * lax.rsqrt(jnp.mean(xc * xc, axis=-1, keepdims=True) + EPS) * g + b


def _dot(a, b):
    return jnp.dot(a, b, preferred_element_type=F32)


def _resident(block_shape, index_map):
    return pl.BlockSpec(block_shape, index_map, pipeline_mode=pl.Buffered(1))


def _whole(a, lead=()):
    nd = a.ndim - len(lead)
    return _resident((None,) * len(lead) + a.shape[len(lead):], lambda t: tuple(lead) + (0,) * nd)


def _skewed_cols(cols):
    return cols + LANES if cols % (SKEW_PERIOD * LANES) == 0 else cols


class _Cast:
    def __init__(self, w, lead, n_prompt):
        rows, cols = w.shape[-2:]
        share = 1
        while n_prompt % share or rows % (n_prompt // share) or (rows // (n_prompt // share)) % BF16_ROWS:
            share *= 2
            assert share <= n_prompt
        n_blocks = n_prompt // share
        blk = rows // n_blocks

        def block(t):
            return jnp.minimum(t // share, n_blocks - 1)

        self.operand = w
        self.in_spec = pl.BlockSpec((None,) * len(lead) + (blk, cols), lambda t: tuple(lead) + (block(t), 0))
        self.out_spec = pl.BlockSpec((blk, _skewed_cols(cols)), lambda t: (block(t), 0))
        self.out_shape = jax.ShapeDtypeStruct((rows, _skewed_cols(cols)), BF16)


def _run_casts(cast_in, cast_out):
    for src, dst in zip(cast_in, cast_out):
        rows, cols = src.shape
        dst[:, :cols] = src[...].astype(BF16)
        if dst.shape[1] > cols:
            dst[:, cols:] = jnp.zeros((rows, dst.shape[1] - cols), BF16)


def _ffn_body(*refs, stages, two_in, two_out, n_cast, n_prompt):
    it = iter(refs)
    xp_ref = next(it)
    xs_ref = next(it) if two_in else None
    ng_ref = next(it)
    weights = [(next(it), next(it)) for _ in stages]
    cast_in = [next(it) for _ in range(n_cast)]
    op_ref = next(it)
    os_ref = next(it) if two_out else None
    cast_out = [next(it) for _ in range(n_cast)]

    t = pl.program_id(0)
    _run_casts(cast_in, cast_out)
    x = xp_ref[...]
    if two_in:
        x = jnp.where(t < n_prompt, x, xs_ref[...])
    for (layer, which), (wgu_ref, wdown_ref) in zip(stages, weights):
        d_ff = wdown_ref.shape[0]
        ng = ng_ref[layer]
        xg, r = _split_norm(x, ng[4 * which:4 * which + 1])
        acc = None
        for c0 in range(0, d_ff, FFN_COL_CHUNK):
            c1 = min(c0 + FFN_COL_CHUNK, d_ff)
            g = _dot(xg, wgu_ref[:, c0:c1]) * r
            u = _dot(xg, wgu_ref[:, d_ff + c0:d_ff + c1]) * r
            a = (g * jax.nn.sigmoid(g) * u).astype(BF16)
            part = _dot(a, wdown_ref[c0:c1, :x.shape[1]])
            acc = part if acc is None else acc + part
        x = x + _rms(acc, 0.5 * ng[4 * which + 1:4 * which + 2])
    if two_out:
        @pl.when(t < n_prompt)
        def _():
            op_ref[...] = x

        @pl.when(t == n_prompt)
        def _():
            os_ref[...] = x
    else:
        op_ref[...] = x


def _ffn(xs_in, norm_g, stages, weights, casts, n_prompt, two_out):
    two_in = len(xs_in) == 2
    d = xs_in[0].shape[1]
    rows = (n_prompt + 1) * ROW_TILE
    tile = pl.BlockSpec((ROW_TILE, d), lambda t: (t, 0))
    prompt_tile = pl.BlockSpec((ROW_TILE, d), lambda t: (jnp.minimum(t, n_prompt - 1), 0))
    sample_tile = pl.BlockSpec((ROW_TILE, d), lambda t: (0, 0))
    body = functools.partial(_ffn_body, stages=tuple(stages), two_in=two_in, two_out=two_out,
                             n_cast=len(casts), n_prompt=n_prompt)
    if two_out:
        x_out_specs = [prompt_tile, sample_tile]
        x_out_shapes = [jax.ShapeDtypeStruct((n_prompt * ROW_TILE, d), F32),
                        jax.ShapeDtypeStruct((ROW_TILE, d), F32)]
    else:
        x_out_specs = [tile]
        x_out_shapes = [jax.ShapeDtypeStruct((rows, d), F32)]
    flat_weights = [w for pair in weights for w in pair]
    return pl.pallas_call(
        body,
        grid=(n_prompt + 1,),
        in_specs=([prompt_tile, sample_tile] if two_in else [tile])
        + [_whole(norm_g)] + [_whole(w) for w in flat_weights] + [c.in_spec for c in casts],
        out_specs=x_out_specs + [c.out_spec for c in casts],
        out_shape=x_out_shapes + [c.out_shape for c in casts],
        compiler_params=pltpu.CompilerParams(
            dimension_semantics=("arbitrary",), vmem_limit_bytes=VMEM_LIMIT_BYTES),
        name="ffn" + "".join(f"_l{layer}w{which}" for layer, which in stages),
    )(*xs_in, norm_g, *flat_weights, *[c.operand for c in casts])


def _project(h, win_ref, lo, hi):
    return _dot(h, win_ref[:, lo:hi])


def _join(parts):
    return jnp.concatenate(parts, axis=-1)


def _mixer_prompt_tile(j, is_last_j, x_ref, ng_ref, win_ref, ws_ref, bst_ref, lng_ref, lnb_ref, wconv_ref,
                       wpool_ref, pscale_ref, wa_ref, wb_ref, wc_ref, wo_ref,
                       o_ref, conv_out_ref, pool_out_ref, zext_ref, pext_ref, ya_ref, psum_ref,
                       *, side_jobs, ts, d_model, d_br):
    gw = d_br // N_GROUPS

    @pl.when(j == 0)
    def _():
        zext_ref[0:HALO, :] = jnp.zeros((HALO, d_br), F32)
        pext_ref[0:HALO, :] = jnp.zeros((HALO, d_br), F32)
        psum_ref[:, 0:SUBLANES, :] = jnp.zeros((N_GROUPS, SUBLANES, gw), F32)

    side_jobs()
    x = x_ref[...]
    ng = ng_ref[...]
    h = _rms(x, ng[2:3]).astype(BF16)

    def slabs(lo, hi):
        return [functools.partial(_project, h, win_ref, c, c + MXU_COLS) for c in range(lo, hi, MXU_COLS)]

    v_raw = _project(h, win_ref, d_br, 2 * d_br)
    u_raw = _project(h, win_ref, 0, d_br)
    u = jax.nn.gelu(u_raw)
    vb = _layer_norm(jax.nn.gelu(v_raw), lng_ref[...], lnb_ref[...]).astype(BF16)

    rows_i = lax.broadcasted_iota(jnp.int32, (CHUNK, CHUNK), 0)
    cols_i = lax.broadcasted_iota(jnp.int32, (CHUNK, CHUNK), 1)
    causal = cols_i <= rows_i
    n_chunks = ts // CHUNK

    def spatial(g):
        c0 = g * gw
        wsg = jnp.where(causal, ws_ref[g], 0.0).astype(BF16)
        mixed = _dot(wsg, _join([vb[n * CHUNK:(n + 1) * CHUNK, c0:c0 + gw] for n in range(n_chunks)]))
        bias = bst_ref[:, g:g + 1]
        for n in range(n_chunks):
            r0 = n * CHUNK
            ya_ref[r0:r0 + CHUNK, c0:c0 + gw] = (u[r0:r0 + CHUNK, c0:c0 + gw]
                                                * (mixed[:, n * gw:(n + 1) * gw] + bias))

    big = slabs(2 * d_br, 6 * d_br + 2 * d_model)
    done = []
    for k, slab in enumerate(big):
        done.append(slab())
        if k % 2 == 1 and k // 2 < N_GROUPS:
            spatial(k // 2)
    per = d_br // MXU_COLS
    xb, gb, gc, p = (_join(done[i * per:(i + 1) * per]) for i in range(4))
    per_gate = d_model // MXU_COLS
    gate_a = jax.nn.sigmoid(_join(done[4 * per:4 * per + per_gate]))
    gate_b = jax.nn.sigmoid(_join(done[4 * per + per_gate:]))
    merged = gate_a * _dot(ya_ref[...].astype(BF16), wa_ref[:, :d_model])

    z = gc * xb
    zext_ref[HALO:HALO + ts, :] = z
    wconv = wconv_ref[...]
    conv = wconv[0:1] * zext_ref[HALO - 2:HALO - 2 + ts, :]
    conv = conv + wconv[1:2] * zext_ref[HALO - 1:HALO - 1 + ts, :]
    conv = conv + wconv[2:3] * z
    yb = (gb * conv).astype(BF16)

    pext_ref[HALO:HALO + ts, :] = p
    pos = j * ts + lax.broadcasted_iota(jnp.int32, (ts, gw), 0)
    d_groups = []
    for g, w in enumerate(POOL_WINDOWS):
        c0 = g * gw
        run = pext_ref[:, c0:c0 + gw]
        shift = 1
        while shift < w:
            psum_ref[g, SUBLANES:, :] = run
            run = run + psum_ref[g, SUBLANES - shift:SUBLANES - shift + HALO + ts, :]
            shift *= 2
        cnt = jnp.minimum(pos + 1, w).astype(F32)
        d_groups.append(run[HALO:] / cnt - p[:, c0:c0 + gw])
    gate_c_parts, yc_parts = [], []
    zero_block = jnp.zeros((gw, gw), BF16)
    for k, slab in enumerate(slabs(6 * d_br + 2 * d_model, 6 * d_br + 3 * d_model)):
        gate_c_parts.append(slab())
        if k % 2 == 0:
            w_pair = jnp.concatenate([_join([wpool_ref[k], zero_block]),
                                      _join([zero_block, wpool_ref[k + 1]])], axis=0)
            yc_parts.append(_dot(_join(d_groups[k:k + 2]).astype(BF16), w_pair))
    yc = _join(yc_parts) * pscale_ref[...]

    merged = merged + gate_b * _dot(yb, wb_ref[:, :d_model])
    merged = merged + jax.nn.sigmoid(_join(gate_c_parts)) * _dot(yc.astype(BF16), wc_ref[:, :d_model])

    out = _dot(merged.astype(BF16), wo_ref[:, :d_model])
    o_ref[...] = x + _rms(out, ng[3:4])

    @pl.when(is_last_j)
    def _():
        conv_out_ref[...] = zext_ref[HALO + ts - (CONV_W - 1):HALO + ts, :]
        pool_out_ref[...] = pext_ref[HALO + ts - POOL_BUF:HALO + ts, :]

    zext_ref[0:HALO, :] = zext_ref[ts:ts + HALO, :]
    pext_ref[0:HALO, :] = pext_ref[ts:ts + HALO, :]


def _mixer_sample_tile(ws_ref, bs_ref, x_ref, conv_ref, pool_ref, ng_ref, win_ref, lng_ref, lnb_ref,
                       wconv_ref, wpool_ref, pscale_ref, wa_ref, wb_ref, wc_ref, wo_ref,
                       o_ref, v_out_ref, z_out_ref, p_out_ref,
                       *, side_jobs, layer, nb, steps, d_model, d_br, start_pos):
    gw = d_br // N_GROUPS
    side_jobs()
    x = x_ref[...]
    ng = ng_ref[...]
    h = _rms(x, ng[2:3]).astype(BF16)

    def blk(a, i):
        return a[i * nb:(i + 1) * nb]

    u = jax.nn.gelu(_project(h, win_ref, 0, d_br))
    v = _layer_norm(jax.nn.gelu(_project(h, win_ref, d_br, 2 * d_br)), lng_ref[...], lnb_ref[...])
    v_out_ref[...] = v
    ya_rows = []
    for t in range(steps):
        cols = []
        for g in range(N_GROUPS):
            c0 = g * gw
            mixed = None
            for s in range(t + 1):
                term = ws_ref[layer, g * steps * steps + t * steps + s] * blk(v, s)[:, c0:c0 + gw]
                mixed = term if mixed is None else mixed + term
            cols.append(mixed + bs_ref[layer, g * steps + t])
        ya_rows.append(blk(u, t) * _join(cols))
    ya = jnp.concatenate(ya_rows, axis=0)

    z = _project(h, win_ref, 4 * d_br, 5 * d_br) * _project(h, win_ref, 2 * d_br, 3 * d_br)
    z_out_ref[...] = z
    z_ext = [conv_ref[k] for k in range(CONV_W - 1)] + [blk(z, i) for i in range(steps)]
    wconv = wconv_ref[...]
    conv_rows = []
    for i in range(steps):
        c = wconv[0:1] * z_ext[i]
        for k in range(1, CONV_W):
            c = c + wconv[k:k + 1] * z_ext[i + k]
        conv_rows.append(c)
    yb = _project(h, win_ref, 3 * d_br, 4 * d_br) * jnp.concatenate(conv_rows, axis=0)

    p = _project(h, win_ref, 5 * d_br, 6 * d_br)
    p_out_ref[...] = p
    p_ext = [pool_ref[k] for k in range(POOL_BUF)] + [blk(p, i) for i in range(steps)]
    d_groups = []
    for g, w in enumerate(POOL_WINDOWS):
        c0 = g * gw
        rows_g = []
        for i in range(steps):
            s = p_ext[POOL_BUF + i][:, c0:c0 + gw]
            for k in range(1, w):
                s = s + p_ext[POOL_BUF + i - k][:, c0:c0 + gw]
            cnt = float(min(start_pos + i + 1, w))
            rows_g.append(s / cnt - p_ext[POOL_BUF + i][:, c0:c0 + gw])
        d_groups.append(jnp.concatenate(rows_g, axis=0))
    yc = _join([_dot(d_groups[g].astype(BF16), wpool_ref[g]) for g in range(N_GROUPS)]) * pscale_ref[...]

    merged = None
    for k, (y, w_ref) in enumerate(((ya, wa_ref), (yb, wb_ref), (yc, wc_ref))):
        lo = 6 * d_br + k * d_model
        term = jax.nn.sigmoid(_project(h, win_ref, lo, lo + d_model)) * _dot(y.astype(BF16), w_ref[:, :d_model])
        merged = term if merged is None else merged + term
    out = _dot(merged.astype(BF16), wo_ref[:, :d_model])
    o_ref[...] = x + _rms(out, ng[3:4])


def _mixer_body(*refs, layer, n_prompt, nj, n_cast, ts, nb, steps, d_model, d_br):
    it = iter(refs)
    ws_sm, bs_sm, x_ref, conv_s_ref, pool_s_ref = (next(it) for _ in range(5))
    (ng_ref, win_ref, ws_ref, bst_ref, lng_ref, lnb_ref, wconv_ref, wpool_ref, pscale_ref,
     wa_ref, wb_ref, wc_ref, wo_ref) = (next(it) for _ in range(13))
    cast_in = [next(it) for _ in range(n_cast)]
    o_ref, conv_out_ref, pool_out_ref, v_out_ref, z_out_ref, p_out_ref = (next(it) for _ in range(6))
    cast_out = [next(it) for _ in range(n_cast)]
    zext_ref, pext_ref, ya_ref, psum_ref = next(it), next(it), next(it), next(it)

    t = pl.program_id(0)
    side_jobs = functools.partial(_run_casts, cast_in, cast_out)

    @pl.when(t < n_prompt)
    def _():
        j = lax.rem(t, nj)
        _mixer_prompt_tile(j, j == nj - 1, x_ref, ng_ref, win_ref, ws_ref, bst_ref, lng_ref, lnb_ref,
                           wconv_ref, wpool_ref, pscale_ref, wa_ref, wb_ref, wc_ref, wo_ref,
                           o_ref, conv_out_ref, pool_out_ref, zext_ref, pext_ref, ya_ref, psum_ref,
                           side_jobs=side_jobs, ts=ts, d_model=d_model, d_br=d_br)

    @pl.when(t == n_prompt)
    def _():
        _mixer_sample_tile(ws_sm, bs_sm, x_ref, conv_s_ref, pool_s_ref, ng_ref, win_ref, lng_ref, lnb_ref,
                           wconv_ref, wpool_ref, pscale_ref, wa_ref, wb_ref, wc_ref, wo_ref,
                           o_ref, v_out_ref, z_out_ref, p_out_ref, side_jobs=side_jobs,
                           layer=layer, nb=nb, steps=steps, d_model=d_model, d_br=d_br, start_pos=PAST_LEN)


def _mixer(x, batch, steps, conv_s, pool_s, ws_small, bs_small, norm_g, win, w_s, b_s_t, ln_g, ln_b,
           w_conv, wpool, pool_scale, wa, wb, wc, wo, casts, layer, n_prompt):
    rows, d_model = x.shape
    d_br = wa.shape[0]
    ts = ROW_TILE
    nj = n_prompt // batch
    nb = ROW_TILE // steps
    assert n_prompt % batch == 0 and ts % CHUNK == 0 and ts >= HALO
    assert steps <= CHUNK and nb * steps == ROW_TILE and nb % SUBLANES == 0
    assert d_br % MXU_COLS == 0 and d_model // MXU_COLS == N_GROUPS
    body = functools.partial(_mixer_body, layer=layer, n_prompt=n_prompt, nj=nj, n_cast=len(casts),
                             ts=ts, nb=nb, steps=steps, d_model=d_model, d_br=d_br)
    smem = pl.BlockSpec(memory_space=pltpu.SMEM)
    tile = pl.BlockSpec((ts, d_model), lambda t: (t, 0))
    seq_of = lambda t: jnp.minimum(t // nj, batch - 1)
    sample_rows = pl.BlockSpec((ROW_TILE, d_br), lambda t: (0, 0))
    layer_ops = (w_s, b_s_t, ln_g, ln_b, w_conv, wpool, pool_scale)
    out_proj = (wa, wb, wc, wo)
    return pl.pallas_call(
        body,
        grid=(n_prompt + 1,),
        in_specs=[smem, smem, tile, _whole(conv_s, (layer,)), _whole(pool_s, (layer,)),
                  _whole(norm_g, (layer,)), _whole(win)]
        + [_whole(a, (layer,)) for a in layer_ops] + [_whole(a) for a in out_proj]
        + [c.in_spec for c in casts],
        out_specs=[
            tile,
            pl.BlockSpec((None, CONV_W - 1, d_br), lambda t: (seq_of(t), 0, 0)),
            pl.BlockSpec((None, POOL_BUF, d_br), lambda t: (seq_of(t), 0, 0)),
            sample_rows, sample_rows, sample_rows,
        ] + [c.out_spec for c in casts],
        out_shape=[
            jax.ShapeDtypeStruct((rows, d_model), F32),
            jax.ShapeDtypeStruct((batch, CONV_W - 1, d_br), F32),
            jax.ShapeDtypeStruct((batch, POOL_BUF, d_br), F32),
            jax.ShapeDtypeStruct((ROW_TILE, d_br), F32),
            jax.ShapeDtypeStruct((ROW_TILE, d_br), F32),
            jax.ShapeDtypeStruct((ROW_TILE, d_br), F32),
        ] + [c.out_shape for c in casts],
        scratch_shapes=[
            pltpu.VMEM((HALO + ts, d_br), F32),
            pltpu.VMEM((HALO + ts, d_br), F32),
            pltpu.VMEM((ts, d_br), F32),
            pltpu.VMEM((N_GROUPS, SUBLANES + HALO + ts, d_br // N_GROUPS), F32),
        ],
        compiler_params=pltpu.CompilerParams(
            dimension_semantics=("arbitrary",), vmem_limit_bytes=VMEM_LIMIT_BYTES),
        name=f"mixer_l{layer}",
    )(ws_small, bs_small, x, conv_s, pool_s, norm_g, win, *layer_ops, *out_proj,
      *[c.operand for c in casts])


def kernel(x_prompt, x_sample, state_conv, state_pool, norm_g, w_ffn_gu, w_ffn_down, w_in, w_s, b_s,
           ln_g, ln_b, w_conv, w_pool, pool_scale, w_a_out, w_b_out, w_c_out, w_o):
    depth = w_in.shape[0]
    batch, seq, d_model = x_prompt.shape
    nb, steps, _ = x_sample.shape
    assert (batch * seq) % ROW_TILE == 0 and nb * steps == ROW_TILE
    n_prompt = batch * seq // ROW_TILE

    wpool = w_pool.astype(BF16)
    lng, lnb, pscale = (a[:, None, :] for a in (ln_g, ln_b, pool_scale))
    b_s_t = jnp.swapaxes(b_s, 1, 2)
    ws_small = w_s[:, :, :steps, :steps].reshape(depth, -1)
    bs_small = b_s[:, :, :steps].reshape(depth, -1)
    conv_s = jnp.swapaxes(state_conv, 1, 2)
    pool_s = jnp.swapaxes(state_pool, 1, 2)

    x_in = (x_prompt.reshape(batch * seq, d_model),
            jnp.swapaxes(x_sample, 0, 1).reshape(steps * nb, d_model))
    w_ffn = (w_ffn_gu[0, 0].astype(BF16), w_ffn_down[0, 0].astype(BF16))

    def ffn_casts(layer, which):
        return [_Cast(w_ffn_gu, (layer, which), n_prompt), _Cast(w_ffn_down, (layer, which), n_prompt)]

    conv_p_out, pool_p_out, conv_s_out, pool_s_out, v_s_out = [], [], [], [], []
    for l in range(depth):
        mixer_casts = [_Cast(w, (l,), n_prompt) for w in (w_in, w_a_out, w_b_out, w_c_out, w_o)]
        x, win, wa, wb, wc, wo = _ffn(x_in, norm_g, [(l, 0)], [w_ffn], mixer_casts, n_prompt, False)
        x, cp, pp, vs, zs, ps, *w_ffn = _mixer(
            x, batch, steps, conv_s, pool_s, ws_small, bs_small, norm_g, win, w_s, b_s_t, lng, lnb,
            w_conv, wpool, pscale, wa, wb, wc, wo, ffn_casts(l, 1), l, n_prompt)
        if l + 1 < depth:
            x, *w_ffn = _ffn((x,), norm_g, [(l, 1)], [w_ffn], ffn_casts(l + 1, 0), n_prompt, False)
            x_in = (x,)
        else:
            y_prompt_rows, y_sample_rows = _ffn((x,), norm_g, [(l, 1)], [w_ffn], [], n_prompt, True)
        conv_p_out.append(cp)
        pool_p_out.append(pp)
        z_ext = jnp.concatenate([conv_s[l], zs.reshape(steps, nb, -1)], axis=0)[-(CONV_W - 1):]
        p_ext = jnp.concatenate([pool_s[l], ps.reshape(steps, nb, -1)], axis=0)[-POOL_BUF:]
        conv_s_out.append(jnp.swapaxes(z_ext, 0, 1))
        pool_s_out.append(jnp.swapaxes(p_ext, 0, 1))
        v_s_out.append(jnp.swapaxes(vs.reshape(steps, nb, -1), 0, 1))

    y_prompt = y_prompt_rows.reshape(batch, seq, d_model)
    y_sample = jnp.swapaxes(y_sample_rows.reshape(steps, nb, d_model), 0, 1)
    return (y_prompt, y_sample, jnp.stack(conv_p_out), jnp.stack(pool_p_out),
            jnp.stack(conv_s_out), jnp.stack(pool_s_out), jnp.stack(v_s_out))
```

```python
import functools

import jax
import jax.numpy as jnp
from jax import lax
from jax.experimental import pallas as pl
from jax.experimental.pallas import tpu as pltpu

EPS = 1e-6
PAST_LEN = 16384
CHUNK = 128
N_GROUPS = 4
POOL_WINDOWS = (2, 4, 8, 16)
POOL_BUF = max(POOL_WINDOWS) - 1
CONV_W = 3

SUBLANES = 8
LANES = 128
BF16_ROWS = 16
SKEW_PERIOD = 8
HALO = 16
FFN_COL_CHUNK = 512
ROW_TILE = 512
MXU_COLS = 256
MIB = 1024 * 1024
FFN_VMEM_LIMIT_BYTES = 44 * MIB
MIXER_VMEM_LIMIT_BYTES = 60 * MIB

F32 = jnp.float32
BF16 = jnp.bfloat16


def _rms(x, g):
    return x * lax.rsqrt(jnp.mean(x * x, axis=-1, keepdims=True) + EPS) * g


def _split_norm(x, g):
    return (x * g).astype(BF16), lax.rsqrt(jnp.mean(x * x, axis=-1, keepdims=True) + EPS)


def _layer_norm(x, g, b):
    xc = x - jnp.mean(x, axis=-1, keepdims=True)
    return xc * lax.rsqrt(jnp.mean(xc * xc, axis=-1, keepdims=True) + EPS) * g + b


def _dot(a, b):
    return jnp.dot(a, b, preferred_element_type=F32)


def _resident(block_shape, index_map):
    return pl.BlockSpec(block_shape, index_map, pipeline_mode=pl.Buffered(1))


def _whole(a, lead=()):
    nd = a.ndim - len(lead)
    return _resident((None,) * len(lead) + a.shape[len(lead):], lambda t: tuple(lead) + (0,) * nd)


def _skewed_cols(cols):
    return cols + LANES if cols % (SKEW_PERIOD * LANES) == 0 else cols


class _Cast:
    def __init__(self, w, lead, n_prompt):
        rows, cols = w.shape[-2:]
        share = 1
        while n_prompt % share or rows % (n_prompt // share) or (rows // (n_prompt // share)) % BF16_ROWS:
            share *= 2
            assert share <= n_prompt
        n_blocks = n_prompt // share
        blk = rows // n_blocks

        def block(t):
            return jnp.minimum(t // share, n_blocks - 1)

        self.operand = w
        self.in_spec = pl.BlockSpec((None,) * len(lead) + (blk, cols), lambda t: tuple(lead) + (block(t), 0))
        self.out_spec = pl.BlockSpec((blk, _skewed_cols(cols)), lambda t: (block(t), 0))
        self.out_shape = jax.ShapeDtypeStruct((rows, _skewed_cols(cols)), BF16)


def _run_casts(cast_in, cast_out):
    for src, dst in zip(cast_in, cast_out):
        rows, cols = src.shape
        dst[:, :cols] = src[...].astype(BF16)
        if dst.shape[1] > cols:
            dst[:, cols:] = jnp.zeros((rows, dst.shape[1] - cols), BF16)


def _ffn_body(*refs, stages, two_in, two_out, n_cast, n_prompt):
    it = iter(refs)
    xp_ref = next(it)
    xs_ref = next(it) if two_in else None
    ng_ref = next(it)
    weights = [(next(it), next(it)) for _ in stages]
    cast_in = [next(it) for _ in range(n_cast)]
    op_ref = next(it)
    os_ref = next(it) if two_out else None
    cast_out = [next(it) for _ in range(n_cast)]

    t = pl.program_id(0)
    x = xp_ref[...]
    if two_in:
        x = jnp.where(t < n_prompt, x, xs_ref[...])
    for (layer, which), (wgu_ref, wdown_ref) in zip(stages, weights):
        d_ff = wdown_ref.shape[0]
        ng = ng_ref[layer]
        xg, r = _split_norm(x, ng[4 * which:4 * which + 1])
        acc = None
        for c0 in range(0, d_ff, FFN_COL_CHUNK):
            c1 = min(c0 + FFN_COL_CHUNK, d_ff)
            g = _dot(xg, wgu_ref[:, c0:c1]) * r
            u = _dot(xg, wgu_ref[:, d_ff + c0:d_ff + c1]) * r
            a = (g * jax.nn.sigmoid(g) * u).astype(BF16)
            part = _dot(a, wdown_ref[c0:c1, :x.shape[1]])
            acc = part if acc is None else acc + part
        x = x + _rms(acc, 0.5 * ng[4 * which + 1:4 * which + 2])
    if two_out:
        @pl.when(t < n_prompt)
        def _():
            op_ref[...] = x

        @pl.when(t == n_prompt)
        def _():
            os_ref[...] = x
    else:
        op_ref[...] = x
    _run_casts(cast_in, cast_out)


def _ffn(xs_in, norm_g, stages, weights, casts, n_prompt, two_out):
    two_in = len(xs_in) == 2
    d = xs_in[0].shape[1]
    rows = (n_prompt + 1) * ROW_TILE
    tile = pl.BlockSpec((ROW_TILE, d), lambda t: (t, 0))
    prompt_tile = pl.BlockSpec((ROW_TILE, d), lambda t: (jnp.minimum(t, n_prompt - 1), 0))
    sample_tile = pl.BlockSpec((ROW_TILE, d), lambda t: (0, 0))
    body = functools.partial(_ffn_body, stages=tuple(stages), two_in=two_in, two_out=two_out,
                             n_cast=len(casts), n_prompt=n_prompt)
    if two_out:
        x_out_specs = [prompt_tile, sample_tile]
        x_out_shapes = [jax.ShapeDtypeStruct((n_prompt * ROW_TILE, d), F32),
                        jax.ShapeDtypeStruct((ROW_TILE, d), F32)]
    else:
        x_out_specs = [tile]
        x_out_shapes = [jax.ShapeDtypeStruct((rows, d), F32)]
    flat_weights = [w for pair in weights for w in pair]
    return pl.pallas_call(
        body,
        grid=(n_prompt + 1,),
        in_specs=([prompt_tile, sample_tile] if two_in else [tile])
        + [_whole(norm_g)] + [_whole(w) for w in flat_weights] + [c.in_spec for c in casts],
        out_specs=x_out_specs + [c.out_spec for c in casts],
        out_shape=x_out_shapes + [c.out_shape for c in casts],
        compiler_params=pltpu.CompilerParams(
            dimension_semantics=("arbitrary",), vmem_limit_bytes=FFN_VMEM_LIMIT_BYTES),
        name="ffn" + "".join(f"_l{layer}w{which}" for layer, which in stages),
    )(*xs_in, norm_g, *flat_weights, *[c.operand for c in casts])


def _project(h, win_ref, lo, hi):
    return _dot(h, win_ref[:, lo:hi])


def _join(parts):
    return jnp.concatenate(parts, axis=-1)


def _mixer_prompt_tile(j, is_last_j, x_ref, ng_ref, win_ref, ws_ref, bst_ref, lng_ref, lnb_ref, wconv_ref,
                       wpool_ref, pscale_ref, wa_ref, wb_ref, wc_ref, wo_ref,
                       o_ref, conv_out_ref, pool_out_ref, zext_ref, pext_ref, ya_ref, psum_ref,
                       *, ts, d_model, d_br):
    gw = d_br // N_GROUPS

    @pl.when(j == 0)
    def _():
        zext_ref[0:HALO, :] = jnp.zeros((HALO, d_br), F32)
        pext_ref[0:HALO, :] = jnp.zeros((HALO, d_br), F32)
        psum_ref[:, 0:SUBLANES, :] = jnp.zeros((N_GROUPS, SUBLANES, gw), F32)

    x = x_ref[...]
    ng = ng_ref[...]
    h = _rms(x, ng[2:3]).astype(BF16)

    def slabs(lo, hi):
        return [functools.partial(_project, h, win_ref, c, c + MXU_COLS) for c in range(lo, hi, MXU_COLS)]

    v_raw = _project(h, win_ref, d_br, 2 * d_br)
    u_raw = _project(h, win_ref, 0, d_br)
    u = jax.nn.gelu(u_raw)
    vb = _layer_norm(jax.nn.gelu(v_raw), lng_ref[...], lnb_ref[...]).astype(BF16)

    rows_i = lax.broadcasted_iota(jnp.int32, (CHUNK, CHUNK), 0)
    cols_i = lax.broadcasted_iota(jnp.int32, (CHUNK, CHUNK), 1)
    causal = cols_i <= rows_i
    n_chunks = ts // CHUNK

    def spatial(g):
        c0 = g * gw
        wsg = jnp.where(causal, ws_ref[g], 0.0).astype(BF16)
        mixed = _dot(wsg, _join([vb[n * CHUNK:(n + 1) * CHUNK, c0:c0 + gw] for n in range(n_chunks)]))
        bias = bst_ref[:, g:g + 1]
        for n in range(n_chunks):
            r0 = n * CHUNK
            ya_ref[r0:r0 + CHUNK, c0:c0 + gw] = (u[r0:r0 + CHUNK, c0:c0 + gw]
                                                * (mixed[:, n * gw:(n + 1) * gw] + bias))

    big = slabs(2 * d_br, 6 * d_br + 2 * d_model)
    done = []
    for k, slab in enumerate(big):
        done.append(slab())
        if k % 2 == 1 and k // 2 < N_GROUPS:
            spatial(k // 2)
    per = d_br // MXU_COLS
    xb, gb, gc, p = (_join(done[i * per:(i + 1) * per]) for i in range(4))
    per_gate = d_model // MXU_COLS
    gate_a = jax.nn.sigmoid(_join(done[4 * per:4 * per + per_gate]))
    gate_b = jax.nn.sigmoid(_join(done[4 * per + per_gate:]))
    merged = gate_a * _dot(ya_ref[...].astype(BF16), wa_ref[:, :d_model])

    z = gc * xb
    zext_ref[HALO:HALO + ts, :] = z
    wconv = wconv_ref[...]
    conv = wconv[0:1] * zext_ref[HALO - 2:HALO - 2 + ts, :]
    conv = conv + wconv[1:2] * zext_ref[HALO - 1:HALO - 1 + ts, :]
    conv = conv + wconv[2:3] * z
    yb = (gb * conv).astype(BF16)

    pext_ref[HALO:HALO + ts, :] = p
    pos = j * ts + lax.broadcasted_iota(jnp.int32, (ts, gw), 0)
    d_groups = []
    for g, w in enumerate(POOL_WINDOWS):
        c0 = g * gw
        run = pext_ref[:, c0:c0 + gw]
        shift = 1
        while shift < w:
            psum_ref[g, SUBLANES:, :] = run
            run = run + psum_ref[g, SUBLANES - shift:SUBLANES - shift + HALO + ts, :]
            shift *= 2
        cnt = jnp.minimum(pos + 1, w).astype(F32)
        d_groups.append(run[HALO:] / cnt - p[:, c0:c0 + gw])
    gate_c_parts, yc_parts = [], []
    zero_block = jnp.zeros((gw, gw), BF16)
    for k, slab in enumerate(slabs(6 * d_br + 2 * d_model, 6 * d_br + 3 * d_model)):
        gate_c_parts.append(slab())
        if k % 2 == 0:
            w_pair = jnp.concatenate([_join([wpool_ref[k], zero_block]),
                                      _join([zero_block, wpool_ref[k + 1]])], axis=0)
            yc_parts.append(_dot(_join(d_groups[k:k + 2]).astype(BF16), w_pair))
    yc = _join(yc_parts) * pscale_ref[...]

    merged = merged + gate_b * _dot(yb, wb_ref[:, :d_model])
    merged = merged + jax.nn.sigmoid(_join(gate_c_parts)) * _dot(yc.astype(BF16), wc_ref[:, :d_model])

    out = _dot(merged.astype(BF16), wo_ref[:, :d_model])
    o_ref[...] = x + _rms(out, ng[3:4])

    @pl.when(is_last_j)
    def _():
        conv_out_ref[...] = zext_ref[HALO + ts - (CONV_W - 1):HALO + ts, :]
        pool_out_ref[...] = pext_ref[HALO + ts - POOL_BUF:HALO + ts, :]

    zext_ref[0:HALO, :] = zext_ref[ts:ts + HALO, :]
    pext_ref[0:HALO, :] = pext_ref[ts:ts + HALO, :]


def _mixer_sample_tile(ws_ref, bs_ref, x_ref, conv_ref, pool_ref, ng_ref, win_ref, lng_ref, lnb_ref,
                       wconv_ref, wpool_ref, pscale_ref, wa_ref, wb_ref, wc_ref, wo_ref,
                       o_ref, v_out_ref, z_out_ref, p_out_ref,
                       *, layer, nb, steps, d_model, d_br, start_pos):
    gw = d_br // N_GROUPS
    x = x_ref[...]
    ng = ng_ref[...]
    h = _rms(x, ng[2:3]).astype(BF16)

    def blk(a, i):
        return a[i * nb:(i + 1) * nb]

    u = jax.nn.gelu(_project(h, win_ref, 0, d_br))
    v = _layer_norm(jax.nn.gelu(_project(h, win_ref, d_br, 2 * d_br)), lng_ref[...], lnb_ref[...])
    v_out_ref[...] = v
    ya_rows = []
    for t in range(steps):
        cols = []
        for g in range(N_GROUPS):
            c0 = g * gw
            mixed = None
            for s in range(t + 1):
                term = ws_ref[layer, g * steps * steps + t * steps + s] * blk(v, s)[:, c0:c0 + gw]
                mixed = term if mixed is None else mixed + term
            cols.append(mixed + bs_ref[layer, g * steps + t])
        ya_rows.append(blk(u, t) * _join(cols))
    ya = jnp.concatenate(ya_rows, axis=0)

    z = _project(h, win_ref, 4 * d_br, 5 * d_br) * _project(h, win_ref, 2 * d_br, 3 * d_br)
    z_out_ref[...] = z
    z_ext = [conv_ref[k] for k in range(CONV_W - 1)] + [blk(z, i) for i in range(steps)]
    wconv = wconv_ref[...]
    conv_rows = []
    for i in range(steps):
        c = wconv[0:1] * z_ext[i]
        for k in range(1, CONV_W):
            c = c + wconv[k:k + 1] * z_ext[i + k]
        conv_rows.append(c)
    yb = _project(h, win_ref, 3 * d_br, 4 * d_br) * jnp.concatenate(conv_rows, axis=0)

    p = _project(h, win_ref, 5 * d_br, 6 * d_br)
    p_out_ref[...] = p
    p_ext = [pool_ref[k] for k in range(POOL_BUF)] + [blk(p, i) for i in range(steps)]
    d_groups = []
    for g, w in enumerate(POOL_WINDOWS):
        c0 = g * gw
        rows_g = []
        for i in range(steps):
            s = p_ext[POOL_BUF + i][:, c0:c0 + gw]
            for k in range(1, w):
                s = s + p_ext[POOL_BUF + i - k][:, c0:c0 + gw]
            cnt = float(min(start_pos + i + 1, w))
            rows_g.append(s / cnt - p_ext[POOL_BUF + i][:, c0:c0 + gw])
        d_groups.append(jnp.concatenate(rows_g, axis=0))
    yc = _join([_dot(d_groups[g].astype(BF16), wpool_ref[g]) for g in range(N_GROUPS)]) * pscale_ref[...]

    merged = None
    for k, (y, w_ref) in enumerate(((ya, wa_ref), (yb, wb_ref), (yc, wc_ref))):
        lo = 6 * d_br + k * d_model
        term = jax.nn.sigmoid(_project(h, win_ref, lo, lo + d_model)) * _dot(y.astype(BF16), w_ref[:, :d_model])
        merged = term if merged is None else merged + term
    out = _dot(merged.astype(BF16), wo_ref[:, :d_model])
    o_ref[...] = x + _rms(out, ng[3:4])


def _mixer_body(*refs, layer, n_prompt, nj, n_cast, ts, nb, steps, d_model, d_br):
    it = iter(refs)
    ws_sm, bs_sm, x_ref, conv_s_ref, pool_s_ref = (next(it) for _ in range(5))
    (ng_ref, win_ref, ws_ref, bst_ref, lng_ref, lnb_ref, wconv_ref, wpool_ref, pscale_ref,
     wa_ref, wb_ref, wc_ref, wo_ref) = (next(it) for _ in range(13))
    cast_in = [next(it) for _ in range(n_cast)]
    o_ref, conv_out_ref, pool_out_ref, v_out_ref, z_out_ref, p_out_ref = (next(it) for _ in range(6))
    cast_out = [next(it) for _ in range(n_cast)]
    zext_ref, pext_ref, ya_ref, psum_ref = next(it), next(it), next(it), next(it)

    t = pl.program_id(0)

    @pl.when(t < n_prompt)
    def _():
        j = lax.rem(t, nj)
        _mixer_prompt_tile(j, j == nj - 1, x_ref, ng_ref, win_ref, ws_ref, bst_ref, lng_ref, lnb_ref,
                           wconv_ref, wpool_ref, pscale_ref, wa_ref, wb_ref, wc_ref, wo_ref,
                           o_ref, conv_out_ref, pool_out_ref, zext_ref, pext_ref, ya_ref, psum_ref,
                           ts=ts, d_model=d_model, d_br=d_br)

    @pl.when(t == n_prompt)
    def _():
        _mixer_sample_tile(ws_sm, bs_sm, x_ref, conv_s_ref, pool_s_ref, ng_ref, win_ref, lng_ref, lnb_ref,
                           wconv_ref, wpool_ref, pscale_ref, wa_ref, wb_ref, wc_ref, wo_ref,
                           o_ref, v_out_ref, z_out_ref, p_out_ref,
                           layer=layer, nb=nb, steps=steps, d_model=d_model, d_br=d_br, start_pos=PAST_LEN)

    _run_casts(cast_in, cast_out)


def _mixer(x, batch, steps, conv_s, pool_s, ws_small, bs_small, norm_g, win, w_s, b_s_t, ln_g, ln_b,
           w_conv, wpool, pool_scale, wa, wb, wc, wo, casts, layer, n_prompt):
    rows, d_model = x.shape
    d_br = wa.shape[0]
    ts = ROW_TILE
    nj = n_prompt // batch
    nb = ROW_TILE // steps
    assert n_prompt % batch == 0 and ts % CHUNK == 0 and ts >= HALO
    assert steps <= CHUNK and nb * steps == ROW_TILE and nb % SUBLANES == 0
    assert d_br % MXU_COLS == 0 and d_model // MXU_COLS == N_GROUPS
    body = functools.partial(_mixer_body, layer=layer, n_prompt=n_prompt, nj=nj, n_cast=len(casts),
                             ts=ts, nb=nb, steps=steps, d_model=d_model, d_br=d_br)
    smem = pl.BlockSpec(memory_space=pltpu.SMEM)
    tile = pl.BlockSpec((ts, d_model), lambda t: (t, 0))
    seq_of = lambda t: jnp.minimum(t // nj, batch - 1)
    sample_rows = pl.BlockSpec((ROW_TILE, d_br), lambda t: (0, 0))
    layer_ops = (w_s, b_s_t, ln_g, ln_b, w_conv, wpool, pool_scale)
    out_proj = (wa, wb, wc, wo)
    return pl.pallas_call(
        body,
        grid=(n_prompt + 1,),
        in_specs=[smem, smem, tile, _whole(conv_s, (layer,)), _whole(pool_s, (layer,)),
                  _whole(norm_g, (layer,)), _whole(win)]
        + [_whole(a, (layer,)) for a in layer_ops] + [_whole(a) for a in out_proj]
        + [c.in_spec for c in casts],
        out_specs=[
            tile,
            pl.BlockSpec((None, CONV_W - 1, d_br), lambda t: (seq_of(t), 0, 0)),
            pl.BlockSpec((None, POOL_BUF, d_br), lambda t: (seq_of(t), 0, 0)),
            sample_rows, sample_rows, sample_rows,
        ] + [c.out_spec for c in casts],
        out_shape=[
            jax.ShapeDtypeStruct((rows, d_model), F32),
            jax.ShapeDtypeStruct((batch, CONV_W - 1, d_br), F32),
            jax.ShapeDtypeStruct((batch, POOL_BUF, d_br), F32),
            jax.ShapeDtypeStruct((ROW_TILE, d_br), F32),
            jax.ShapeDtypeStruct((ROW_TILE, d_br), F32),
            jax.ShapeDtypeStruct((ROW_TILE, d_br), F32),
        ] + [c.out_shape for c in casts],
        scratch_shapes=[
            pltpu.VMEM((HALO + ts, d_br), F32),
            pltpu.VMEM((HALO + ts, d_br), F32),
            pltpu.VMEM((ts, d_br), F32),
            pltpu.VMEM((N_GROUPS, SUBLANES + HALO + ts, d_br // N_GROUPS), F32),
        ],
        compiler_params=pltpu.CompilerParams(
            dimension_semantics=("arbitrary",), vmem_limit_bytes=MIXER_VMEM_LIMIT_BYTES),
        name=f"mixer_l{layer}",
    )(ws_small, bs_small, x, conv_s, pool_s, norm_g, win, *layer_ops, *out_proj,
      *[c.operand for c in casts])


def kernel(x_prompt, x_sample, state_conv, state_pool, norm_g, w_ffn_gu, w_ffn_down, w_in, w_s, b_s,
           ln_g, ln_b, w_conv, w_pool, pool_scale, w_a_out, w_b_out, w_c_out, w_o):
    depth = w_in.shape[0]
    batch, seq, d_model = x_prompt.shape
    nb, steps, _ = x_sample.shape
    assert (batch * seq) % ROW_TILE == 0 and nb * steps == ROW_TILE
    n_prompt = batch * seq // ROW_TILE

    wpool = w_pool.astype(BF16)
    lng, lnb, pscale = (a[:, None, :] for a in (ln_g, ln_b, pool_scale))
    b_s_t = jnp.swapaxes(b_s, 1, 2)
    ws_small = w_s[:, :, :steps, :steps].reshape(depth, -1)
    bs_small = b_s[:, :, :steps].reshape(depth, -1)
    conv_s = jnp.swapaxes(state_conv, 1, 2)
    pool_s = jnp.swapaxes(state_pool, 1, 2)

    x_in = (x_prompt.reshape(batch * seq, d_model),
            jnp.swapaxes(x_sample, 0, 1).reshape(steps * nb, d_model))
    w_ffn = (w_ffn_gu[0, 0].astype(BF16), w_ffn_down[0, 0].astype(BF16))

    def ffn_casts(layer, which):
        return [_Cast(w_ffn_gu, (layer, which), n_prompt), _Cast(w_ffn_down, (layer, which), n_prompt)]

    conv_p_out, pool_p_out, conv_s_out, pool_s_out, v_s_out = [], [], [], [], []
    for l in range(depth):
        mixer_casts = [_Cast(w, (l,), n_prompt) for w in (w_in, w_a_out, w_b_out, w_c_out, w_o)]
        x, win, wa, wb, wc, wo = _ffn(x_in, norm_g, [(l, 0)], [w_ffn], mixer_casts, n_prompt, False)
        x, cp, pp, vs, zs, ps, *w_ffn = _mixer(
            x, batch, steps, conv_s, pool_s, ws_small, bs_small, norm_g, win, w_s, b_s_t, lng, lnb,
            w_conv, wpool, pscale, wa, wb, wc, wo, ffn_casts(l, 1), l, n_prompt)
        if l + 1 < depth:
            x, *w_ffn = _ffn((x,), norm_g, [(l, 1)], [w_ffn], ffn_casts(l + 1, 0), n_prompt, False)
            x_in = (x,)
        else:
            y_prompt_rows, y_sample_rows = _ffn((x,), norm_g, [(l, 1)], [w_ffn], [], n_prompt, True)
        conv_p_out.append(cp)
        pool_p_out.append(pp)
        z_ext = jnp.concatenate([conv_s[l], zs.reshape(steps, nb, -1)], axis=0)[-(CONV_W - 1):]
        p_ext = jnp.concatenate([pool_s[l], ps.reshape(steps, nb, -1)], axis=0)[-POOL_BUF:]
        conv_s_out.append(jnp.swapaxes(z_ext, 0, 1))
        pool_s_out.append(jnp.swapaxes(p_ext, 0, 1))
        v_s_out.append(jnp.swapaxes(vs.reshape(steps, nb, -1), 0, 1))

    y_prompt = y_prompt_rows.reshape(batch, seq, d_model)
    y_sample = jnp.swapaxes(y_sample_rows.reshape(steps, nb, d_model), 0, 1)
    return (y_prompt, y_sample, jnp.stack(conv_p_out), jnp.stack(pool_p_out),
            jnp.stack(conv_s_out), jnp.stack(pool_s_out), jnp.stack(v_s_out))
```
